```python
import jax, jax.numpy as jnp
from jax import lax
import numpy as np

D_MODEL = 1024
BATCH = 4
SEQ = 8192
DEPTH = 1

D_MIX = D_MODEL
RET_HEADS = 4
RET_HEAD_DIM = 128
RET_W = RET_HEADS * RET_HEAD_DIM
GLA_HEADS = 4
GLA_KEY_DIM = 64
GLA_VALUE_DIM = 128
GLA_KW = GLA_HEADS * GLA_KEY_DIM
GLA_VW = GLA_HEADS * GLA_VALUE_DIM
GLA_GATE_RANK = 16
GLA_GATE_NORMALIZER = 16.0
RET_CHUNK = 128
GLA_CHUNK = 64
D_FF = 2816
CONV_WIDTH = 3
ROPE_BASE = 10000.0
EPS = 1e-6
SPLITS = (RET_W, RET_W, RET_W, RET_W, GLA_KW, GLA_KW, GLA_VW, GLA_VW, GLA_GATE_RANK)
D_IN_PROJ = RET_W * 4 + GLA_KW * 2 + GLA_VW * 2 + GLA_GATE_RANK

kernel_name = "hybrid_retention_gla_convffn"


def rms_norm(x, w):
    xf = x.astype(jnp.float32)
    y = xf * lax.rsqrt(jnp.mean(xf * xf, axis=-1, keepdims=True) + EPS)
    return (y * w.astype(jnp.float32)).astype(x.dtype)


def rotary(t, positions):
    half = t.shape[-1] // 2
    inv_freq = ROPE_BASE ** (-jnp.arange(half, dtype=jnp.float32) / half)
    ang = positions.astype(jnp.float32)[..., None] * inv_freq
    cos = jnp.cos(ang)[:, :, None, :]
    sin = jnp.sin(ang)[:, :, None, :]
    tf = t.astype(jnp.float32)
    t1, t2 = tf[..., :half], tf[..., half:]
    return jnp.concatenate([t1 * cos - t2 * sin, t2 * cos + t1 * sin], axis=-1)


def to_chunks(t, c):
    b, s, h, d = t.shape
    return t.reshape(b, s // c, c, h, d).transpose(1, 0, 3, 2, 4)


def from_chunks(o):
    n, b, h, c, d = o.shape
    return o.transpose(1, 0, 3, 2, 4).reshape(b, n * c, h, d)


def retention_chunkwise(q, k, v):
    b, s, h, dk = q.shape
    dv = v.shape[-1]
    c = RET_CHUNK
    log_gamma = jnp.log(1.0 - 2.0 ** (-5.0 - jnp.arange(h, dtype=jnp.float32)))
    j = jnp.arange(c, dtype=jnp.float32)
    diff = j[:, None] - j[None, :]
    intra = jnp.where(diff >= 0, jnp.exp(log_gamma[:, None, None] * jnp.maximum(diff, 0.0)), 0.0)
    q_dec = jnp.exp(log_gamma[:, None] * (j + 1.0))[:, :, None]
    k_dec = jnp.exp(log_gamma[:, None] * (c - 1.0 - j))[:, :, None]
    chunk_dec = jnp.exp(log_gamma * c)[:, None, None]

    def step(state, inp):
        qc, kc, vc = inp
        scores = jnp.einsum('bhjd,bhld->bhjl', qc, kc) * intra
        o = jnp.einsum('bhjl,bhle->bhje', scores, vc) \
            + jnp.einsum('bhjd,bhde->bhje', qc * q_dec, state)
        state = state * chunk_dec + jnp.einsum('bhld,bhle->bhde', kc * k_dec, vc)
        return state, o

    state0 = jnp.zeros((b, h, dk, dv), jnp.float32)
    _, o = lax.scan(step, state0, (to_chunks(q, c), to_chunks(k, c), to_chunks(v, c)))
    return from_chunks(o)


def gla_chunkwise(q, k, v, log_g):
    b, s, h, dk = q.shape
    dv = v.shape[-1]
    c = GLA_CHUNK
    mask = jnp.tril(jnp.ones((c, c), dtype=bool))[:, :, None]

    def step(state, inp):
        qc, kc, vc, gc = inp
        cum = jnp.cumsum(gc, axis=2)
        pair = cum[:, :, :, None, :] - cum[:, :, None, :, :]
        decay = jnp.exp(jnp.where(mask, pair, -jnp.inf))
        attn = jnp.einsum('bhjd,bhld,bhjld->bhjl', qc, kc, decay)
        o = jnp.einsum('bhjl,bhle->bhje', attn, vc) \
            + jnp.einsum('bhjd,bhde->bhje', qc * jnp.exp(cum), state)
        last = cum[:, :, -1:, :]
        state = state * jnp.swapaxes(jnp.exp(last), -1, -2) \
            + jnp.einsum('bhld,bhle->bhde', kc * jnp.exp(last - cum), vc)
        return state, o

    state0 = jnp.zeros((b, h, dk, dv), jnp.float32)
    _, o = lax.scan(step, state0, (to_chunks(q, c), to_chunks(k, c), to_chunks(v, c), to_chunks(log_g, c)))
    return from_chunks(o)


def causal_depthwise_conv(u, w, bias):
    ch = u.shape[-1]
    y = lax.conv_general_dilated(u, w[:, None, :], window_strides=(1,),
                                 padding=[(CONV_WIDTH - 1, 0)],
                                 dimension_numbers=('NWC', 'WIO', 'NWC'),
                                 feature_group_count=ch)
    return y + bias


def setup_inputs(seed: int = 0) -> dict:
    key = jax.random.key(seed)
    ks = jax.random.split(key, 16)
    f32 = jnp.float32
    L = DEPTH
    nrm = lambda k, shape, scale: jax.random.normal(k, shape, f32) * scale
    return {
        "x": jax.random.normal(ks[0], (BATCH, SEQ, D_MODEL), f32),
        "positions": jnp.broadcast_to(jnp.arange(SEQ, dtype=jnp.int32), (BATCH, SEQ)),
        "norm1_w": 1.0 + nrm(ks[1], (L, D_MODEL), 0.02),
        "w_in": nrm(ks[2], (L, D_MODEL, D_IN_PROJ), D_MODEL ** -0.5),
        "ret_norm_w": 1.0 + nrm(ks[3], (L, RET_W), 0.02),
        "ret_norm_b": nrm(ks[4], (L, RET_W), 0.02),
        "gla_gate_w2": nrm(ks[5], (L, GLA_GATE_RANK, GLA_KW), GLA_GATE_RANK ** -0.5),
        "gla_gate_b": nrm(ks[6], (L, GLA_KW), 0.02),
        "gla_norm_w": 1.0 + nrm(ks[7], (L, GLA_VW), 0.02),
        "w_out": nrm(ks[8], (L, D_MIX, D_MODEL), D_MIX ** -0.5),
        "norm2_w": 1.0 + nrm(ks[9], (L, D_MODEL), 0.02),
        "ffn_w_up": nrm(ks[10], (L, D_MODEL, 2 * D_FF), D_MODEL ** -0.5),
        "ffn_conv_w": nrm(ks[11], (L, CONV_WIDTH, 2 * D_FF), CONV_WIDTH ** -0.5),
        "ffn_conv_b": nrm(ks[12], (L, 2 * D_FF), 0.02),
        "ffn_w_down": nrm(ks[13], (L, D_FF, D_MODEL), D_FF ** -0.5),
        "final_norm_w": 1.0 + nrm(ks[14], (D_MODEL,), 0.02),
    }


def reference(x, positions, norm1_w, w_in, ret_norm_w, ret_norm_b, gla_gate_w2, gla_gate_b,
              gla_norm_w, w_out, norm2_w, ffn_w_up, ffn_conv_w, ffn_conv_b, ffn_w_down, final_norm_w):
    b, s, _ = x.shape
    offsets = []
    acc = 0
    for width in SPLITS[:-1]:
        acc += width
        offsets.append(acc)
    heads = lambda t, n: t.reshape(b, s, n, -1)
    h = x
    for layer in range(DEPTH):
        xn = rms_norm(h, norm1_w[layer])
        proj = xn @ w_in[layer]
        rq, rk, rv, rg, gq, gk, gv, gg, g_low = jnp.split(proj, offsets, axis=-1)

        q_r = rotary(heads(rq, RET_HEADS), positions)
        k_r = rotary(heads(rk, RET_HEADS), positions) * (RET_HEAD_DIM ** -0.5)
        o_r = retention_chunkwise(q_r, k_r, heads(rv, RET_HEADS).astype(jnp.float32))
        mu = jnp.mean(o_r, axis=-1, keepdims=True)
        var = jnp.mean(jnp.square(o_r - mu), axis=-1, keepdims=True)
        o_r = ((o_r - mu) * lax.rsqrt(var + EPS)).reshape(b, s, RET_W)
        o_r = (o_r * ret_norm_w[layer].astype(jnp.float32) + ret_norm_b[layer].astype(jnp.float32)) \
            * jax.nn.silu(rg.astype(jnp.float32))

        gate_logits = (g_low @ gla_gate_w2[layer] + gla_gate_b[layer]).astype(jnp.float32)
        log_g = jax.nn.log_sigmoid(gate_logits) / GLA_GATE_NORMALIZER
        q_g = heads(gq, GLA_HEADS).astype(jnp.float32) * (GLA_KEY_DIM ** -0.5)
        k_g = heads(gk, GLA_HEADS).astype(jnp.float32)
        v_g = heads(gv, GLA_HEADS).astype(jnp.float32)
        o_g = gla_chunkwise(q_g, k_g, v_g, heads(log_g, GLA_HEADS))
        o_g = (o_g * lax.rsqrt(jnp.mean(o_g * o_g, axis=-1, keepdims=True) + EPS)).reshape(b, s, GLA_VW)
        o_g = o_g * gla_norm_w[layer].astype(jnp.float32) * jax.nn.silu(gg.astype(jnp.float32))

        mixed = jnp.concatenate([o_r, o_g], axis=-1).astype(h.dtype)
        h = h + mixed @ w_out[layer]

        xn2 = rms_norm(h, norm2_w[layer])
        u = causal_depthwise_conv(xn2 @ ffn_w_up[layer], ffn_conv_w[layer], ffn_conv_b[layer])
        u_gate, u_val = jnp.split(u, 2, axis=-1)
        h = h + (jax.nn.silu(u_gate) * u_val) @ ffn_w_down[layer]
    return rms_norm(h, final_norm_w)
```

```python
import functools
import math

import numpy as np
import jax
import jax.numpy as jnp
from jax import lax
from jax.experimental import pallas as pl
from jax.experimental.pallas import tpu as pltpu

F32 = jnp.float32
BF16 = jnp.bfloat16

LANES = 128
EPS = 1e-6
ROPE_BASE = 10000.0

RET_HEADS = 4
RET_DIM = 128
RET_W = RET_HEADS * RET_DIM
RET_CHUNK = 128
GLA_HEADS = 4
GLA_KEY_DIM = 64
GLA_VALUE_DIM = 128
GLA_KW = GLA_HEADS * GLA_KEY_DIM
GLA_VW = GLA_HEADS * GLA_VALUE_DIM
GLA_PAIRS = GLA_HEADS // 2
GLA_RANK = 16
GLA_GATE_NORMALIZER = 16.0
GLA_CHUNK = 64
GLA_SUB = 16
CONV_WIDTH = 3

OFF_RQ, OFF_RK, OFF_RV, OFF_RG = 0, 512, 1024, 1536
OFF_GQ, OFF_GK, OFF_GV, OFF_GG, OFF_GLOW = 2048, 2304, 2560, 3072, 3584
D_IN_PROJ = OFF_GLOW + GLA_RANK
D_IN_PAD = OFF_GLOW + LANES

PROJ_ROWS = 256
MIX_ROWS = 256
MIX_BLOCK = 128
FFN_ROWS = 256
FFN_COLS = 256
CARRY_ROWS = 8

VMEM_LIMIT = 56 * 1024 * 1024


def _silu(v):
    return v * (1.0 / (1.0 + jnp.exp(-v)))


def _proj_kernel(x_ref, pos_ref, n1w_ref, w_ref, w2_ref, gb_ref, freq_ref,
                 qr_ref, kr_ref, rv_ref, rgate_ref, gq_ref, gk_ref, cum_ref, gv_ref, ggate_ref):
    rows = x_ref.shape[0]
    x = x_ref[...]
    ms = jnp.mean(x * x, axis=-1, keepdims=True)
    xn = (x * lax.rsqrt(ms + EPS) * n1w_ref[...]).astype(BF16)

    def seg(lo, hi):
        return jnp.dot(xn, w_ref[:, lo:hi], preferred_element_type=F32)

    ang = pos_ref[...].astype(F32) * freq_ref[0:1, :] - freq_ref[1:2, :]
    cs = jnp.cos(ang)
    sc = pltpu.roll(cs, RET_DIM // 2, 1)
    low = lax.broadcasted_iota(jnp.int32, (rows, LANES), 1) < RET_DIM // 2
    cos_full = jnp.where(low, cs, sc)
    sin_signed = jnp.where(low, -sc, cs)

    def rotary(t):
        return t * cos_full + pltpu.roll(t, RET_DIM // 2, 1) * sin_signed

    rq = seg(OFF_RQ, OFF_RK)
    rk = seg(OFF_RK, OFF_RV)
    for h in range(RET_HEADS):
        sl = slice(h * RET_DIM, (h + 1) * RET_DIM)
        qr_ref[:, sl] = rotary(rq[:, sl]).astype(BF16)
        kr_ref[:, sl] = (rotary(rk[:, sl]) * (RET_DIM ** -0.5)).astype(BF16)
    rv_ref[...] = seg(OFF_RV, OFF_RG).astype(BF16)
    rgate_ref[...] = _silu(seg(OFF_RG, OFF_GQ))

    gq_ref[...] = seg(OFF_GQ, OFF_GK) * (GLA_KEY_DIM ** -0.5)
    gk_ref[...] = seg(OFF_GK, OFF_GV)
    gv_ref[...] = seg(OFF_GV, OFF_GG).astype(BF16)
    ggate_ref[...] = _silu(seg(OFF_GG, OFF_GLOW))

    g_low = seg(OFF_GLOW, D_IN_PAD).astype(BF16)
    logits = jnp.dot(g_low, w2_ref[...], preferred_element_type=F32) + gb_ref[...]
    log_g = (jnp.minimum(logits, 0.0) - jnp.log1p(jnp.exp(-jnp.abs(logits)))) * (1.0 / GLA_GATE_NORMALIZER)
    row = lax.broadcasted_iota(jnp.int32, (rows, GLA_KW), 0) & (GLA_CHUNK - 1)
    c = log_g
    step = 1
    while step < GLA_CHUNK:
        c = c + jnp.where(row >= step, pltpu.roll(c, step, 0), 0.0)
        step *= 2
    cum_ref[...] = c


def _dot_nt(a, b):
    return lax.dot_general(a, b, (((1,), (1,)), ((), ())), preferred_element_type=F32)


def _dot_tn(a, b):
    return lax.dot_general(a, b, (((0,), (0,)), ((), ())), preferred_element_type=F32)


def _bcast_rows(parts):
    return jnp.concatenate([jnp.broadcast_to(p, (GLA_SUB, LANES)) for p in parts], axis=0)


def _mix_kernel(qr_ref, kr_ref, rv_ref, rgate_ref, gq_ref, gk_ref, cum_ref, gv_ref, ggate_ref,
                rtab_ref, rcdec_ref, gmask_ref, rnw_ref, rnb_ref, gnw_ref,
                out_ref, rstate_ref, gstate_ref):
    @pl.when(pl.program_id(1) == 0)
    def _():
        rstate_ref[...] = jnp.zeros_like(rstate_ref)
        gstate_ref[...] = jnp.zeros_like(gstate_ref)

    lane = lax.broadcasted_iota(jnp.int32, (GLA_CHUNK, LANES), 1)
    head0 = lane < GLA_KEY_DIM
    vlane = lax.broadcasted_iota(jnp.int32, (GLA_CHUNK, 2 * GLA_VALUE_DIM), 1)
    vhead0 = vlane < GLA_VALUE_DIM
    st_row = lax.broadcasted_iota(jnp.int32, (2 * GLA_VALUE_DIM, LANES), 0)
    st_lane = lax.broadcasted_iota(jnp.int32, (2 * GLA_VALUE_DIM, LANES), 1)
    st_diag = (st_row < GLA_VALUE_DIM) == (st_lane < GLA_KEY_DIM)
    zero_row = jnp.zeros((1, LANES), F32)

    def retention_head(h, r0):
        rows = pl.ds(r0, RET_CHUNK)
        sl = slice(h * RET_DIM, (h + 1) * RET_DIM)
        q = qr_ref[rows, sl]
        k = kr_ref[rows, sl]
        v = rv_ref[rows, sl]
        state = rstate_ref[h]
        scores = (_dot_nt(q, k) * rtab_ref[h, 0]).astype(BF16)
        q_dec = (q.astype(F32) * rtab_ref[h, 1]).astype(BF16)
        o = (jnp.dot(scores, v, preferred_element_type=F32)
             + jnp.dot(q_dec, state.astype(BF16), preferred_element_type=F32))
        k_dec = (k.astype(F32) * rtab_ref[h, 2]).astype(BF16)
        rstate_ref[h] = state * rcdec_ref[h] + _dot_tn(k_dec, v)
        mu = jnp.mean(o, axis=-1, keepdims=True)
        d = o - mu
        var = jnp.mean(d * d, axis=-1, keepdims=True)
        y = (d * lax.rsqrt(var + EPS)) * rnw_ref[:, sl] + rnb_ref[:, sl]
        out_ref[rows, sl] = (y * rgate_ref[rows, sl]).astype(out_ref.dtype)

    def gla_pair(p, r0):
        rows = pl.ds(r0, GLA_CHUNK)
        ksl = slice(p * LANES, (p + 1) * LANES)
        vsl = slice(p * 2 * GLA_VALUE_DIM, (p + 1) * 2 * GLA_VALUE_DIM)
        q = gq_ref[rows, ksl]
        k = gk_ref[rows, ksl]
        c = cum_ref[rows, ksl]
        v = gv_ref[rows, vsl]
        ends = [c[(m + 1) * GLA_SUB - 1:(m + 1) * GLA_SUB, :] for m in range(GLA_CHUNK // GLA_SUB)]
        last = ends[-1]
        ref_prev1 = _bcast_rows([zero_row, ends[0], ends[1], ends[2]])
        ref_prev2 = _bcast_rows([zero_row, zero_row, ends[0], ends[1]])
        ref_prev3 = _bcast_rows([zero_row, zero_row, zero_row, ends[0]])
        ref_own = _bcast_rows(ends)
        q1 = q * jnp.exp(c - ref_prev1)
        q2 = q * jnp.exp(c - ref_prev2)
        q3 = q * jnp.exp(c - ref_prev3)
        q_in = q * jnp.exp(c)
        k_band = (k * jnp.exp(ref_own - c)).astype(BF16)
        k_diag = (k * jnp.exp(ref_prev1 - c)).astype(BF16)
        k_out = (k * jnp.exp(last - c)).astype(BF16)
        zk = jnp.zeros_like(k_band)
        lhs = jnp.concatenate([q1, q2, q3], axis=0).astype(BF16)
        rhs = jnp.concatenate([jnp.where(head0, k_band, zk), jnp.where(head0, zk, k_band),
                               jnp.where(head0, k_diag, zk), jnp.where(head0, zk, k_diag)], axis=0)
        prod = _dot_nt(lhs, rhs)
        c64 = GLA_CHUNK
        attn = (prod[0:c64, 0:LANES] * gmask_ref[0]
                + prod[c64:2 * c64, 0:LANES] * gmask_ref[1]
                + prod[2 * c64:3 * c64, 0:LANES] * gmask_ref[2]
                + prod[0:c64, LANES:2 * LANES] * gmask_ref[3]).astype(BF16)
        zv = jnp.zeros_like(v)
        v_bd = jnp.concatenate([jnp.where(vhead0, v, zv), jnp.where(vhead0, zv, v)], axis=0)
        state = gstate_ref[p]
        o = (jnp.dot(attn, v_bd, preferred_element_type=F32)
             + _dot_nt(q_in.astype(BF16), state.astype(BF16)))
        upd = _dot_tn(v, k_out)
        gstate_ref[p] = state * jnp.exp(last) + jnp.where(st_diag, upd, 0.0)
        for hh in range(2):
            osl = slice(hh * GLA_VALUE_DIM, (hh + 1) * GLA_VALUE_DIM)
            gsl = slice((2 * p + hh) * GLA_VALUE_DIM, (2 * p + hh + 1) * GLA_VALUE_DIM)
            oh = o[:, osl]
            y = oh * lax.rsqrt(jnp.mean(oh * oh, axis=-1, keepdims=True) + EPS)
            y = y * gnw_ref[:, gsl] * ggate_ref[rows, gsl]
            out_ref[rows, RET_W + gsl.start:RET_W + gsl.stop] = y.astype(out_ref.dtype)

    def block(i, carry):
        r0 = pl.multiple_of(i * MIX_BLOCK, MIX_BLOCK)
        for h in range(RET_HEADS):
            retention_head(h, r0)
        for p in range(GLA_PAIRS):
            for cc in range(MIX_BLOCK // GLA_CHUNK):
                gla_pair(p, r0 + cc * GLA_CHUNK)
        return carry

    lax.fori_loop(0, out_ref.shape[0] // MIX_BLOCK, block, 0)


def _ffn_kernel(x_ref, mixed_ref, wout_ref, n2w_ref, wup_ref, cw_ref, cb_ref, wdown_ref, fw_ref,
                y_ref, carry_ref, act_ref):
    rows = x_ref.shape[0]
    d_ff = wdown_ref.shape[0]

    @pl.when(pl.program_id(1) == 0)
    def _():
        carry_ref[...] = jnp.zeros_like(carry_ref)

    h = x_ref[...] + jnp.dot(mixed_ref[...], wout_ref[...], preferred_element_type=F32)
    ms = jnp.mean(h * h, axis=-1, keepdims=True)
    xn = (h * lax.rsqrt(ms + EPS) * n2w_ref[...]).astype(BF16)

    top = lax.broadcasted_iota(jnp.int32, (CARRY_ROWS, FFN_COLS), 0)

    def conv_block(lo):
        cols = slice(lo, lo + FFN_COLS)
        up = jnp.dot(xn, wup_ref[:, cols], preferred_element_type=F32)
        prev = carry_ref[:, cols]
        carry_ref[:, cols] = up[rows - CARRY_ROWS:, :]
        m1 = pltpu.roll(up, 1, 0)
        m2 = pltpu.roll(up, 2, 0)
        p1 = pltpu.roll(prev, 1, 0)
        p2 = pltpu.roll(prev, 2, 0)
        m1 = jnp.concatenate([jnp.where(top < 1, p1, m1[:CARRY_ROWS]), m1[CARRY_ROWS:]], axis=0)
        m2 = jnp.concatenate([jnp.where(top < 2, p2, m2[:CARRY_ROWS]), m2[CARRY_ROWS:]], axis=0)
        return (cw_ref[0:1, cols] * m2 + cw_ref[1:2, cols] * m1 + cw_ref[2:3, cols] * up) + cb_ref[:, cols]

    for j in range(d_ff // FFN_COLS):
        gate = conv_block(j * FFN_COLS)
        val = conv_block(d_ff + j * FFN_COLS)
        act_ref[:, j * FFN_COLS:(j + 1) * FFN_COLS] = (_silu(gate) * val).astype(BF16)

    h2 = h + jnp.dot(act_ref[...], wdown_ref[...], preferred_element_type=F32)
    ms2 = jnp.mean(h2 * h2, axis=-1, keepdims=True)
    y_ref[...] = (h2 * lax.rsqrt(ms2 + EPS) * fw_ref[...]).astype(y_ref.dtype)


def _retention_tables():
    c = RET_CHUNK
    log_gamma = np.log(1.0 - 2.0 ** (-5.0 - np.arange(RET_HEADS, dtype=np.float64)))
    j = np.arange(c, dtype=np.float64)
    diff = j[:, None] - j[None, :]
    intra = np.where(diff >= 0, np.exp(log_gamma[:, None, None] * np.maximum(diff, 0.0)), 0.0)
    q_dec = np.broadcast_to(np.exp(log_gamma[:, None] * (j + 1.0))[:, :, None], (RET_HEADS, c, c))
    k_dec = np.broadcast_to(np.exp(log_gamma[:, None] * (c - 1.0 - j))[:, :, None], (RET_HEADS, c, c))
    tabs = np.stack([intra, q_dec, k_dec], axis=1).astype(np.float32)
    chunk_dec = np.broadcast_to(np.exp(log_gamma * c)[:, None, None], (RET_HEADS, 1, c)).astype(np.float32)
    return tabs, chunk_dec


def _gla_masks():
    j = np.arange(GLA_CHUNK)[:, None]
    l = np.arange(GLA_CHUNK)[None, :]
    dist = j // GLA_SUB - l // GLA_SUB
    masks = [dist == 1, dist == 2, dist == 3, (dist == 0) & (l <= j)]
    return np.stack([np.concatenate([m, m], axis=1) for m in masks]).astype(np.float32)


def _rotary_table():
    half = RET_DIM // 2
    inv_freq = ROPE_BASE ** (-jnp.arange(half, dtype=F32) / half)
    phase = jnp.concatenate([jnp.zeros((half,), F32), jnp.full((half,), math.pi / 2, F32)])
    return jnp.stack([jnp.concatenate([inv_freq, inv_freq]), phase])


def _const_spec(shape):
    zeros = (0,) * len(shape)
    return pl.BlockSpec(shape, lambda *_: zeros)


def _layer(h, positions, norm1_w, w_in, ret_norm_w, ret_norm_b, gla_gate_w2, gla_gate_b, gla_norm_w,
           w_out, norm2_w, ffn_w_up, ffn_conv_w, ffn_conv_b, ffn_w_down, final_w):
    b, s, d_model = h.shape
    t = b * s
    d_ff = ffn_w_down.shape[0]
    assert s % PROJ_ROWS == 0 and s % MIX_ROWS == 0 and s % FFN_ROWS == 0
    assert d_ff % FFN_COLS == 0 and w_in.shape == (d_model, D_IN_PROJ)

    x2 = h.reshape(t, d_model)
    pos2 = positions.reshape(t, 1)
    w_in_p = jnp.pad(w_in, ((0, 0), (0, D_IN_PAD - D_IN_PROJ))).astype(BF16)
    w2_p = jnp.pad(gla_gate_w2, ((0, LANES - GLA_RANK), (0, 0))).astype(BF16)
    row = lambda a: a.reshape(1, -1).astype(F32)

    def tok(width, rows):
        return pl.BlockSpec((rows, width), lambda i: (i, 0))

    widths = (RET_W, RET_W, RET_W, RET_W, GLA_KW, GLA_KW, GLA_KW, GLA_VW, GLA_VW)
    dtypes = (BF16, BF16, BF16, F32, F32, F32, F32, BF16, F32)
    proj = pl.pallas_call(
        _proj_kernel,
        grid=(t // PROJ_ROWS,),
        in_specs=[tok(d_model, PROJ_ROWS), tok(1, PROJ_ROWS), _const_spec((1, d_model)),
                  _const_spec((d_model, D_IN_PAD)), _const_spec((LANES, GLA_KW)), _const_spec((1, GLA_KW)),
                  _const_spec((2, LANES))],
        out_specs=[tok(w, PROJ_ROWS) for w in widths],
        out_shape=[jax.ShapeDtypeStruct((t, w), dt) for w, dt in zip(widths, dtypes)],
        compiler_params=pltpu.CompilerParams(dimension_semantics=("parallel",), vmem_limit_bytes=VMEM_LIMIT),
        name="proj",
    )(x2, pos2, row(norm1_w), w_in_p, w2_p, row(gla_gate_b), _rotary_table())

    rtab, rcdec = _retention_tables()
    steps = s // MIX_ROWS

    def seq(width):
        return pl.BlockSpec((MIX_ROWS, width), lambda bi, si: (bi * steps + si, 0))

    mixed = pl.pallas_call(
        _mix_kernel,
        grid=(b, steps),
        in_specs=[seq(w) for w in widths]
        + [_const_spec(rtab.shape), _const_spec(rcdec.shape), _const_spec((4, GLA_CHUNK, LANES)),
           _const_spec((1, RET_W)), _const_spec((1, RET_W)), _const_spec((1, GLA_VW))],
        out_specs=seq(RET_W + GLA_VW),
        out_shape=jax.ShapeDtypeStruct((t, RET_W + GLA_VW), BF16),
        scratch_shapes=[pltpu.VMEM((RET_HEADS, RET_DIM, RET_DIM), F32),
                        pltpu.VMEM((GLA_PAIRS, 2 * GLA_VALUE_DIM, LANES), F32)],
        compiler_params=pltpu.CompilerParams(dimension_semantics=("parallel", "arbitrary"),
                                             vmem_limit_bytes=VMEM_LIMIT),
        name="mix",
    )(*proj, jnp.asarray(rtab), jnp.asarray(rcdec), jnp.asarray(_gla_masks()),
      row(ret_norm_w), row(ret_norm_b), row(gla_norm_w))

    fsteps = s // FFN_ROWS

    def ftok(width):
        return pl.BlockSpec((FFN_ROWS, width), lambda bi, si: (bi * fsteps + si, 0))

    def weight(shape):
        zeros = (0,) * len(shape)
        return pl.BlockSpec(shape, lambda *_: zeros, pipeline_mode=pl.Buffered(1))

    y = pl.pallas_call(
        _ffn_kernel,
        grid=(b, fsteps),
        in_specs=[ftok(d_model), ftok(d_model), weight((d_model, d_model)), _const_spec((1, d_model)),
                  weight((d_model, 2 * d_ff)), _const_spec((CONV_WIDTH, 2 * d_ff)), _const_spec((1, 2 * d_ff)),
                  weight((d_ff, d_model)), _const_spec((1, d_model))],
        out_specs=ftok(d_model),
        out_shape=jax.ShapeDtypeStruct((t, d_model), h.dtype),
        scratch_shapes=[pltpu.VMEM((CARRY_ROWS, 2 * d_ff), F32), pltpu.VMEM((FFN_ROWS, d_ff), BF16)],
        compiler_params=pltpu.CompilerParams(dimension_semantics=("parallel", "arbitrary"),
                                             vmem_limit_bytes=VMEM_LIMIT),
        name="ffn",
    )(x2, mixed, w_out.astype(BF16), row(norm2_w), ffn_w_up.astype(BF16), ffn_conv_w.astype(F32),
      row(ffn_conv_b), ffn_w_down.astype(BF16), row(final_w))
    return y.reshape(b, s, d_model)


def kernel(x, positions, norm1_w, w_in, ret_norm_w, ret_norm_b, gla_gate_w2, gla_gate_b, gla_norm_w, w_out,
           norm2_w, ffn_w_up, ffn_conv_w, ffn_conv_b, ffn_w_down, final_norm_w):
    depth = w_in.shape[0]
    assert depth == 1, "the final RMSNorm is fused into the (single) layer's ffn kernel"
    return _layer(x, positions, norm1_w[0], w_in[0], ret_norm_w[0], ret_norm_b[0], gla_gate_w2[0], gla_gate_b[0],
                  gla_norm_w[0], w_out[0], norm2_w[0], ffn_w_up[0], ffn_conv_w[0], ffn_conv_b[0], ffn_w_down[0],
                  final_norm_w)
```

```python
import functools
import math

import numpy as np
import jax
import jax.numpy as jnp
from jax import lax
from jax.experimental import pallas as pl
from jax.experimental.pallas import tpu as pltpu

F32 = jnp.float32
BF16 = jnp.bfloat16

LANES = 128
EPS = 1e-6
ROPE_BASE = 10000.0

RET_HEADS = 4
RET_DIM = 128
RET_W = RET_HEADS * RET_DIM
RET_CHUNK = 128
GLA_HEADS = 4
GLA_KEY_DIM = 64
GLA_VALUE_DIM = 128
GLA_KW = GLA_HEADS * GLA_KEY_DIM
GLA_VW = GLA_HEADS * GLA_VALUE_DIM
GLA_PAIRS = GLA_HEADS // 2
GLA_RANK = 16
GLA_GATE_NORMALIZER = 16.0
GLA_CHUNK = 64
GLA_SUB = 16
CONV_WIDTH = 3

OFF_RQ, OFF_RK, OFF_RV, OFF_RG = 0, 512, 1024, 1536
OFF_GQ, OFF_GK, OFF_GV, OFF_GG, OFF_GLOW = 2048, 2304, 2560, 3072, 3584
D_IN_PROJ = OFF_GLOW + GLA_RANK
D_IN_PAD = OFF_GLOW + LANES

PROJ_ROWS = 512
PROJ_SUB = 256
MIX_ROWS = 128
MIX_BLOCK = 128
MIX_GROUP = 4
FFN_ROWS = 512
FFN_COLS = 256
CARRY_ROWS = 8

VMEM_LIMIT = 56 * 1024 * 1024


def _silu(v):
    return v * (1.0 / (1.0 + jnp.exp(-v)))


def _run_interleaved(chains, stagger=0):
    active = []
    pending = list(chains)
    rounds = 0
    while active or pending:
        while pending and rounds >= stagger * (len(chains) - len(pending)):
            active.append(pending.pop(0))
        active = [c for c in active if next(c, StopIteration) is not StopIteration]
        rounds += 1


def _proj_kernel(x_ref, pos_ref, n1w_ref, w_ref, w2_ref, gb_ref, freq_ref,
                 qr_ref, kr_ref, rv_ref, rgate_ref, gq_ref, gk_ref, cum_ref, gv_ref, ggate_ref):
    def sub_tile(r0):
        rows = pl.ds(r0, PROJ_SUB)
        x = x_ref[rows, :]
        ms = jnp.mean(x * x, axis=-1, keepdims=True)
        xn = (x * lax.rsqrt(ms + EPS) * n1w_ref[...]).astype(BF16)

        def seg(lo, hi):
            return jnp.dot(xn, w_ref[:, lo:hi], preferred_element_type=F32)

        piece = PROJ_SUB // 4
        cs_pieces = []

        def cos_piece():
            prow = pl.ds(r0 + len(cs_pieces) * piece, piece)
            ang = pos_ref[prow, :].astype(F32) * freq_ref[0:1, :] - freq_ref[1:2, :]
            cs_pieces.append(jnp.cos(ang))

        yield
        rv_ref[rows, :] = seg(OFF_RV, OFF_RG).astype(BF16)
        cos_piece()
        yield
        gv_ref[rows, :] = seg(OFF_GV, OFF_GG).astype(BF16)
        cos_piece()
        yield
        g_low = seg(OFF_GLOW, D_IN_PAD).astype(BF16)
        logits = jnp.dot(g_low, w2_ref[...], preferred_element_type=F32) + gb_ref[...]
        log_g = ((jnp.minimum(logits, 0.0) - jnp.log1p(jnp.exp(-jnp.abs(logits))))
                 * (1.0 / GLA_GATE_NORMALIZER))
        cos_piece()
        yield
        rgate_ref[rows, :] = _silu(seg(OFF_RG, OFF_GQ))
        cos_piece()
        yield
        gq_ref[rows, :] = seg(OFF_GQ, OFF_GK) * (GLA_KEY_DIM ** -0.5)
        gk_ref[rows, :] = seg(OFF_GK, OFF_GV)
        row = lax.broadcasted_iota(jnp.int32, (PROJ_SUB, GLA_KW), 0) & (GLA_CHUNK - 1)
        c = log_g
        step = 1
        while step < GLA_CHUNK:
            c = c + jnp.where(row >= step, pltpu.roll(c, step, 0), 0.0)
            step *= 2
        cum_ref[rows, :] = c
        yield
        cs = jnp.concatenate(cs_pieces, axis=0)
        sc = pltpu.roll(cs, RET_DIM // 2, 1)
        low = lax.broadcasted_iota(jnp.int32, (PROJ_SUB, LANES), 1) < RET_DIM // 2
        cos_full = jnp.where(low, cs, sc)
        sin_signed = jnp.where(low, -sc, cs)

        def rotary(t):
            return t * cos_full + pltpu.roll(t, RET_DIM // 2, 1) * sin_signed

        rq = seg(OFF_RQ, OFF_RK)
        for h in range(RET_HEADS):
            sl = slice(h * RET_DIM, (h + 1) * RET_DIM)
            qr_ref[rows, sl] = rotary(rq[:, sl]).astype(BF16)
        yield
        rk = seg(OFF_RK, OFF_RV)
        for h in range(RET_HEADS):
            sl = slice(h * RET_DIM, (h + 1) * RET_DIM)
            kr_ref[rows, sl] = (rotary(rk[:, sl]) * (RET_DIM ** -0.5)).astype(BF16)
        yield
        ggate_ref[rows, :] = _silu(seg(OFF_GG, OFF_GLOW))

    _run_interleaved([sub_tile(r0) for r0 in range(0, x_ref.shape[0], PROJ_SUB)], stagger=1)


def _dot_nt(a, b):
    return lax.dot_general(a, b, (((1,), (1,)), ((), ())), preferred_element_type=F32)


def _dot_tn(a, b):
    return lax.dot_general(a, b, (((0,), (0,)), ((), ())), preferred_element_type=F32)


def _bcast_rows(parts):
    return jnp.concatenate([jnp.broadcast_to(p, (GLA_SUB, LANES)) for p in parts], axis=0)


def _mix_kernel(qr_ref, kr_ref, rv_ref, rgate_ref, gq_ref, gk_ref, cum_ref, gv_ref, ggate_ref,
                rtab_ref, rcdec_ref, gmask_ref, rnw_ref, rnb_ref, gnw_ref,
                out_ref, rstate_ref, gstate_ref):
    @pl.when(pl.program_id(0) == 0)
    def _():
        rstate_ref[...] = jnp.zeros_like(rstate_ref)
        gstate_ref[...] = jnp.zeros_like(gstate_ref)

    lane = lax.broadcasted_iota(jnp.int32, (GLA_CHUNK, LANES), 1)
    head0 = lane < GLA_KEY_DIM
    vlane = lax.broadcasted_iota(jnp.int32, (GLA_CHUNK, 2 * GLA_VALUE_DIM), 1)
    vhead0 = vlane < GLA_VALUE_DIM
    st_row = lax.broadcasted_iota(jnp.int32, (2 * GLA_VALUE_DIM, LANES), 0)
    st_lane = lax.broadcasted_iota(jnp.int32, (2 * GLA_VALUE_DIM, LANES), 1)
    st_diag = (st_row < GLA_VALUE_DIM) == (st_lane < GLA_KEY_DIM)
    zero_row = jnp.zeros((1, LANES), F32)

    def retention_chain(b, h, r0):
        rows = pl.ds(r0, RET_CHUNK)
        sl = slice(h * RET_DIM, (h + 1) * RET_DIM)
        q = qr_ref[b, rows, sl]
        k = kr_ref[b, rows, sl]
        v = rv_ref[b, rows, sl]
        state = rstate_ref[b, h]
        s = _dot_nt(q, k)
        k_dec = (k.astype(F32) * rtab_ref[h, 2]).astype(BF16)
        upd = _dot_tn(k_dec, v)
        yield
        scores = (s * rtab_ref[h, 0]).astype(BF16)
        q_dec = (q.astype(F32) * rtab_ref[h, 1]).astype(BF16)
        o = (jnp.dot(scores, v, preferred_element_type=F32)
             + jnp.dot(q_dec, state.astype(BF16), preferred_element_type=F32))
        rstate_ref[b, h] = state * rcdec_ref[h] + upd
        yield
        mu = jnp.mean(o, axis=-1, keepdims=True)
        yield
        d = o - mu
        var = jnp.mean(d * d, axis=-1, keepdims=True)
        yield
        y = (d * lax.rsqrt(var + EPS)) * rnw_ref[:, sl] + rnb_ref[:, sl]
        out_ref[b, rows, sl] = (y * rgate_ref[b, rows, sl]).astype(out_ref.dtype)

    def gla_chunk_front(b, p, r0):
        rows = pl.ds(r0, GLA_CHUNK)
        ksl = slice(p * LANES, (p + 1) * LANES)
        vsl = slice(p * 2 * GLA_VALUE_DIM, (p + 1) * 2 * GLA_VALUE_DIM)
        q = gq_ref[b, rows, ksl]
        k = gk_ref[b, rows, ksl]
        c = cum_ref[b, rows, ksl]
        v = gv_ref[b, rows, vsl]
        ends = [c[(m + 1) * GLA_SUB - 1:(m + 1) * GLA_SUB, :] for m in range(GLA_CHUNK // GLA_SUB)]
        last = ends[-1]
        ref_prev1 = _bcast_rows([zero_row, ends[0], ends[1], ends[2]])
        ref_prev2 = _bcast_rows([zero_row, zero_row, ends[0], ends[1]])
        ref_prev3 = _bcast_rows([zero_row, zero_row, zero_row, ends[0]])
        ref_own = _bcast_rows(ends)
        q1 = q * jnp.exp(c - ref_prev1)
        q2 = q * jnp.exp(c - ref_prev2)
        q3 = q * jnp.exp(c - ref_prev3)
        q_in = (q * jnp.exp(c)).astype(BF16)
        k_band = (k * jnp.exp(ref_own - c)).astype(BF16)
        k_diag = (k * jnp.exp(ref_prev1 - c)).astype(BF16)
        k_out = (k * jnp.exp(last - c)).astype(BF16)
        zk = jnp.zeros_like(k_band)
        lhs = jnp.concatenate([q1, q2, q3], axis=0).astype(BF16)
        rhs = jnp.concatenate([jnp.where(head0, k_band, zk), jnp.where(head0, zk, k_band),
                               jnp.where(head0, k_diag, zk), jnp.where(head0, zk, k_diag)], axis=0)
        prod = _dot_nt(lhs, rhs)
        upd = _dot_tn(v, k_out)
        zv = jnp.zeros_like(v)
        v_bd = jnp.concatenate([jnp.where(vhead0, v, zv), jnp.where(vhead0, zv, v)], axis=0)
        return dict(rows=rows, prod=prod, upd=upd, v_bd=v_bd, q_in=q_in, decay=jnp.exp(last))

    def gla_chain(b, p, r0):
        chunks = [gla_chunk_front(b, p, r0 + cc * GLA_CHUNK) for cc in range(MIX_BLOCK // GLA_CHUNK)]
        yield
        c64 = GLA_CHUNK
        state = gstate_ref[b, p]
        outs = []
        for ch in chunks:
            prod = ch["prod"]
            attn = (prod[0:c64, 0:LANES] * gmask_ref[0]
                    + prod[c64:2 * c64, 0:LANES] * gmask_ref[1]
                    + prod[2 * c64:3 * c64, 0:LANES] * gmask_ref[2]
                    + prod[0:c64, LANES:2 * LANES] * gmask_ref[3]).astype(BF16)
            outs.append(jnp.dot(attn, ch["v_bd"], preferred_element_type=F32)
                        + _dot_nt(ch["q_in"], state.astype(BF16)))
            state = state * ch["decay"] + jnp.where(st_diag, ch["upd"], 0.0)
        gstate_ref[b, p] = state
        yield
        sums = [[jnp.mean(o[:, hh * GLA_VALUE_DIM:(hh + 1) * GLA_VALUE_DIM] ** 2, axis=-1, keepdims=True)
                 for hh in range(2)] for o in outs]
        yield
        for ch, o, ms in zip(chunks, outs, sums):
            for hh in range(2):
                gsl = slice((2 * p + hh) * GLA_VALUE_DIM, (2 * p + hh + 1) * GLA_VALUE_DIM)
                y = o[:, hh * GLA_VALUE_DIM:(hh + 1) * GLA_VALUE_DIM] * lax.rsqrt(ms[hh] + EPS)
                y = y * gnw_ref[:, gsl] * ggate_ref[b, ch["rows"], gsl]
                out_ref[b, ch["rows"], RET_W + gsl.start:RET_W + gsl.stop] = y.astype(out_ref.dtype)

    n_batch = out_ref.shape[0]
    for r0 in range(0, out_ref.shape[1], MIX_BLOCK):
        for b0 in range(0, n_batch, MIX_GROUP):
            group = []
            for b in range(b0, min(b0 + MIX_GROUP, n_batch)):
                group += [retention_chain(b, h, r0) for h in range(RET_HEADS)]
                group += [gla_chain(b, p, r0) for p in range(GLA_PAIRS)]
            _run_interleaved(group)


def _ffn_kernel(x_ref, mixed_ref, wout_ref, n2w_ref, wup_ref, cw_ref, cb_ref, wdown_ref, fw_ref,
                y_ref, carry_ref, act_ref):
    rows = x_ref.shape[0]
    d_ff = wdown_ref.shape[0]

    @pl.when(pl.program_id(1) == 0)
    def _():
        carry_ref[...] = jnp.zeros_like(carry_ref)

    h = x_ref[...] + jnp.dot(mixed_ref[...], wout_ref[...], preferred_element_type=F32)
    ms = jnp.mean(h * h, axis=-1, keepdims=True)
    xn = (h * lax.rsqrt(ms + EPS) * n2w_ref[...]).astype(BF16)

    top = lax.broadcasted_iota(jnp.int32, (CARRY_ROWS, FFN_COLS), 0)

    def conv_block(lo):
        cols = slice(lo, lo + FFN_COLS)
        up = jnp.dot(xn, wup_ref[:, cols], preferred_element_type=F32)
        prev = carry_ref[:, cols]
        carry_ref[:, cols] = up[rows - CARRY_ROWS:, :]
        m1 = pltpu.roll(up, 1, 0)
        m2 = pltpu.roll(up, 2, 0)
        p1 = pltpu.roll(prev, 1, 0)
        p2 = pltpu.roll(prev, 2, 0)
        m1 = jnp.concatenate([jnp.where(top < 1, p1, m1[:CARRY_ROWS]), m1[CARRY_ROWS:]], axis=0)
        m2 = jnp.concatenate([jnp.where(top < 2, p2, m2[:CARRY_ROWS]), m2[CARRY_ROWS:]], axis=0)
        return (cw_ref[0:1, cols] * m2 + cw_ref[1:2, cols] * m1 + cw_ref[2:3, cols] * up) + cb_ref[:, cols]

    for j in range(d_ff // FFN_COLS):
        gate = conv_block(j * FFN_COLS)
        val = conv_block(d_ff + j * FFN_COLS)
        act_ref[:, j * FFN_COLS:(j + 1) * FFN_COLS] = (_silu(gate) * val).astype(BF16)

    h2 = h + jnp.dot(act_ref[...], wdown_ref[...], preferred_element_type=F32)
    ms2 = jnp.mean(h2 * h2, axis=-1, keepdims=True)
    y_ref[...] = (h2 * lax.rsqrt(ms2 + EPS) * fw_ref[...]).astype(y_ref.dtype)


def _retention_tables():
    c = RET_CHUNK
    log_gamma = np.log(1.0 - 2.0 ** (-5.0 - np.arange(RET_HEADS, dtype=np.float64)))
    j = np.arange(c, dtype=np.float64)
    diff = j[:, None] - j[None, :]
    intra = np.where(diff >= 0, np.exp(log_gamma[:, None, None] * np.maximum(diff, 0.0)), 0.0)
    q_dec = np.broadcast_to(np.exp(log_gamma[:, None] * (j + 1.0))[:, :, None], (RET_HEADS, c, c))
    k_dec = np.broadcast_to(np.exp(log_gamma[:, None] * (c - 1.0 - j))[:, :, None], (RET_HEADS, c, c))
    tabs = np.stack([intra, q_dec, k_dec], axis=1).astype(np.float32)
    chunk_dec = np.broadcast_to(np.exp(log_gamma * c)[:, None, None], (RET_HEADS, 1, c)).astype(np.float32)
    return tabs, chunk_dec


def _gla_masks():
    j = np.arange(GLA_CHUNK)[:, None]
    l = np.arange(GLA_CHUNK)[None, :]
    dist = j // GLA_SUB - l // GLA_SUB
    masks = [dist == 1, dist == 2, dist == 3, (dist == 0) & (l <= j)]
    return np.stack([np.concatenate([m, m], axis=1) for m in masks]).astype(np.float32)


def _rotary_table():
    half = RET_DIM // 2
    inv_freq = ROPE_BASE ** (-jnp.arange(half, dtype=F32) / half)
    phase = jnp.concatenate([jnp.zeros((half,), F32), jnp.full((half,), math.pi / 2, F32)])
    return jnp.stack([jnp.concatenate([inv_freq, inv_freq]), phase])


def _const_spec(shape):
    zeros = (0,) * len(shape)
    return pl.BlockSpec(shape, lambda *_: zeros)


def _layer(h, positions, norm1_w, w_in, ret_norm_w, ret_norm_b, gla_gate_w2, gla_gate_b, gla_norm_w,
           w_out, norm2_w, ffn_w_up, ffn_conv_w, ffn_conv_b, ffn_w_down, final_w):
    b, s, d_model = h.shape
    t = b * s
    d_ff = ffn_w_down.shape[0]
    assert s % PROJ_ROWS == 0 and s % MIX_ROWS == 0 and s % FFN_ROWS == 0
    assert d_ff % FFN_COLS == 0 and w_in.shape == (d_model, D_IN_PROJ)

    x2 = h.reshape(t, d_model)
    pos2 = positions.reshape(t, 1)
    w_in_p = jnp.pad(w_in, ((0, 0), (0, D_IN_PAD - D_IN_PROJ))).astype(BF16)
    w2_p = jnp.pad(gla_gate_w2, ((0, LANES - GLA_RANK), (0, 0))).astype(BF16)
    row = lambda a: a.reshape(1, -1).astype(F32)

    def tok(width, rows):
        return pl.BlockSpec((rows, width), lambda i: (i, 0))

    widths = (RET_W, RET_W, RET_W, RET_W, GLA_KW, GLA_KW, GLA_KW, GLA_VW, GLA_VW)
    dtypes = (BF16, BF16, BF16, F32, F32, F32, F32, BF16, F32)
    proj = pl.pallas_call(
        _proj_kernel,
        grid=(t // PROJ_ROWS,),
        in_specs=[tok(d_model, PROJ_ROWS), tok(1, PROJ_ROWS), _const_spec((1, d_model)),
                  _const_spec((d_model, D_IN_PAD)), _const_spec((LANES, GLA_KW)), _const_spec((1, GLA_KW)),
                  _const_spec((2, LANES))],
        out_specs=[tok(w, PROJ_ROWS) for w in widths],
        out_shape=[jax.ShapeDtypeStruct((t, w), dt) for w, dt in zip(widths, dtypes)],
        compiler_params=pltpu.CompilerParams(dimension_semantics=("parallel",), vmem_limit_bytes=VMEM_LIMIT),
        name="proj",
    )(x2, pos2, row(norm1_w), w_in_p, w2_p, row(gla_gate_b), _rotary_table())

    rtab, rcdec = _retention_tables()
    def seq(width):
        return pl.BlockSpec((b, MIX_ROWS, width), lambda si: (0, si, 0))

    mixed = pl.pallas_call(
        _mix_kernel,
        grid=(s // MIX_ROWS,),
        in_specs=[seq(w) for w in widths]
        + [_const_spec(rtab.shape), _const_spec(rcdec.shape), _const_spec((4, GLA_CHUNK, LANES)),
           _const_spec((1, RET_W)), _const_spec((1, RET_W)), _const_spec((1, GLA_VW))],
        out_specs=seq(RET_W + GLA_VW),
        out_shape=jax.ShapeDtypeStruct((b, s, RET_W + GLA_VW), BF16),
        scratch_shapes=[pltpu.VMEM((b, RET_HEADS, RET_DIM, RET_DIM), F32),
                        pltpu.VMEM((b, GLA_PAIRS, 2 * GLA_VALUE_DIM, LANES), F32)],
        compiler_params=pltpu.CompilerParams(dimension_semantics=("arbitrary",),
                                             vmem_limit_bytes=VMEM_LIMIT),
        name="mix",
    )(*[a.reshape(b, s, -1) for a in proj], jnp.asarray(rtab), jnp.asarray(rcdec), jnp.asarray(_gla_masks()),
      row(ret_norm_w), row(ret_norm_b), row(gla_norm_w))
    mixed = mixed.reshape(t, RET_W + GLA_VW)

    fsteps = s // FFN_ROWS

    def ftok(width):
        return pl.BlockSpec((FFN_ROWS, width), lambda bi, si: (bi * fsteps + si, 0))

    def weight(shape):
        zeros = (0,) * len(shape)
        return pl.BlockSpec(shape, lambda *_: zeros, pipeline_mode=pl.Buffered(1))

    y = pl.pallas_call(
        _ffn_kernel,
        grid=(b, fsteps),
        in_specs=[ftok(d_model), ftok(d_model), weight((d_model, d_model)), _const_spec((1, d_model)),
                  weight((d_model, 2 * d_ff)), _const_spec((CONV_WIDTH, 2 * d_ff)), _const_spec((1, 2 * d_ff)),
                  weight((d_ff, d_model)), _const_spec((1, d_model))],
        out_specs=ftok(d_model),
        out_shape=jax.ShapeDtypeStruct((t, d_model), h.dtype),
        scratch_shapes=[pltpu.VMEM((CARRY_ROWS, 2 * d_ff), F32), pltpu.VMEM((FFN_ROWS, d_ff), BF16)],
        compiler_params=pltpu.CompilerParams(dimension_semantics=("parallel", "arbitrary"),
                                             vmem_limit_bytes=VMEM_LIMIT),
        name="ffn",
    )(x2, mixed, w_out.astype(BF16), row(norm2_w), ffn_w_up.astype(BF16), ffn_conv_w.astype(F32),
      row(ffn_conv_b), ffn_w_down.astype(BF16), row(final_w))
    return y.reshape(b, s, d_model)


def kernel(x, positions, norm1_w, w_in, ret_norm_w, ret_norm_b, gla_gate_w2, gla_gate_b, gla_norm_w, w_out,
           norm2_w, ffn_w_up, ffn_conv_w, ffn_conv_b, ffn_w_down, final_norm_w):
    depth = w_in.shape[0]
    assert depth == 1, "the final RMSNorm is fused into the (single) layer's ffn kernel"
    return _layer(x, positions, norm1_w[0], w_in[0], ret_norm_w[0], ret_norm_b[0], gla_gate_w2[0], gla_gate_b[0],
                  gla_norm_w[0], w_out[0], norm2_w[0], ffn_w_up[0], ffn_conv_w[0], ffn_conv_b[0], ffn_w_down[0],
                  final_norm_w)
```

```python
import math

import numpy as np
import jax
import jax.numpy as jnp
from jax import lax
from jax.experimental import pallas as pl
from jax.experimental.pallas import tpu as pltpu

F32 = jnp.float32
BF16 = jnp.bfloat16

LANES = 128
SUBLANES = 8
EPS = 1e-6
ROPE_BASE = 10000.0
LOG2E = math.log2(math.e)

RET_HEADS = 4
RET_DIM = 128
RET_W = RET_HEADS * RET_DIM
RET_CHUNK = 128
GLA_HEADS = 4
GLA_KEY_DIM = 64
GLA_VALUE_DIM = 128
GLA_KW = GLA_HEADS * GLA_KEY_DIM
GLA_VW = GLA_HEADS * GLA_VALUE_DIM
GLA_PAIRS = GLA_HEADS // 2
GLA_RANK = 16
GLA_GATE_NORMALIZER = 16.0
GLA_CHUNK = 64
GLA_SUB = 16
CONV_WIDTH = 3

OFF_RQ, OFF_RK, OFF_RV, OFF_RG = 0, 512, 1024, 1536
OFF_GQ, OFF_GK, OFF_GV, OFF_GG, OFF_GLOW = 2048, 2304, 2560, 3072, 3584
D_IN_PROJ = OFF_GLOW + GLA_RANK
D_IN_PAD = OFF_GLOW + LANES

PROJ_ROWS = 512
PROJ_CAST_COLS = 512
PROJ_SUB = 256
MIX_ROWS = 128
MIX_BLOCK = 128
MIX_GROUP = 2
MIX_DELAY = 2
FFN_ROWS = 512
FFN_SUB = 256
FFN_COLS = 256
CARRY_ROWS = 8

VMEM_LIMIT = 56 * 1024 * 1024


def _silu(v):
    return v * (1.0 / (1.0 + jnp.exp(-v)))


def _dot_nt(a, b):
    return lax.dot_general(a, b, (((1,), (1,)), ((), ())), preferred_element_type=F32)


def _dot_tn(a, b):
    return lax.dot_general(a, b, (((0,), (0,)), ((), ())), preferred_element_type=F32)


def _run_interleaved(chains, stagger=0):
    active = []
    pending = list(chains)
    rounds = 0
    while active or pending:
        while pending and rounds >= stagger * (len(chains) - len(pending)):
            active.append(pending.pop(0))
        active = [c for c in active if next(c, StopIteration) is not StopIteration]
        rounds += 1


def _zero_after(value):
    bits = lax.bitcast_convert_type(value[0:SUBLANES, 0:LANES], jnp.uint32)
    half = jnp.uint32(16)
    return lax.shift_right_logical(lax.shift_right_logical(bits, half), half).astype(F32)[0:1, :]


def _proj_kernel(x_ref, pos_ref, n1w_ref, w_ref, w2_ref, gb_ref, freq_ref, kdec_ref,
                 qr_ref, kr_ref, kd_ref, rv_ref, rgate_ref, gq_ref, gk_ref, cum_ref, gv_ref, ggate_ref, wb_ref):
    @pl.when(pl.program_id(0) == 0)
    def _():
        for lo in range(0, OFF_GLOW, PROJ_CAST_COLS):
            wb_ref[:, lo:lo + PROJ_CAST_COLS] = w_ref[:, lo:lo + PROJ_CAST_COLS].astype(BF16)
        wb_ref[:, OFF_GLOW:] = jnp.zeros((wb_ref.shape[0], LANES), BF16)
        wb_ref[:, OFF_GLOW:D_IN_PROJ] = w_ref[:, OFF_GLOW:].astype(BF16)

    def sub_tile(r0):
        rows = pl.ds(r0, PROJ_SUB)
        x = x_ref[rows, :]
        ms = jnp.mean(x * x, axis=-1, keepdims=True)
        xn = (x * lax.rsqrt(ms + EPS) * n1w_ref[...]).astype(BF16)

        def seg(lo, hi):
            return jnp.dot(xn, wb_ref[:, lo:hi], preferred_element_type=F32)

        piece = PROJ_SUB // 4
        low = lax.broadcasted_iota(jnp.int32, (piece, LANES), 1) < RET_DIM // 2

        def rotary_piece(i, rq, rk, after):
            prow = pl.ds(r0 + i * piece, piece)
            ang = pos_ref[prow, :].astype(F32) * freq_ref[0:1, :] - freq_ref[1:2, :] + _zero_after(after)
            cs = jnp.cos(ang)
            sc = pltpu.roll(cs, RET_DIM // 2, 1)
            cos_full = jnp.where(low, cs, sc)
            sin_signed = jnp.where(low, -sc, cs)
            for h in range(RET_HEADS):
                sl = slice(h * RET_DIM, (h + 1) * RET_DIM)
                tq = rq[i * piece:(i + 1) * piece, sl]
                tk = rk[i * piece:(i + 1) * piece, sl]
                qr_ref[prow, sl] = (tq * cos_full + pltpu.roll(tq, RET_DIM // 2, 1) * sin_signed).astype(BF16)
                k_rot = (tk * cos_full + pltpu.roll(tk, RET_DIM // 2, 1) * sin_signed) * (RET_DIM ** -0.5)
                kr_ref[prow, sl] = k_rot.astype(BF16)
                kd_ref[prow, sl] = (k_rot * kdec_ref[(i * piece) % RET_CHUNK:(i * piece) % RET_CHUNK + piece, sl]
                                    ).astype(BF16)

        yield
        rq = seg(OFF_RQ, OFF_RK)
        yield
        rk = seg(OFF_RK, OFF_RV)
        yield
        t = seg(OFF_RV, OFF_RG)
        rv_ref[rows, :] = t.astype(BF16)
        rotary_piece(0, rq, rk, t)
        yield
        t = seg(OFF_GV, OFF_GG)
        gv_ref[rows, :] = t.astype(BF16)
        rotary_piece(1, rq, rk, t)
        yield
        g_low = seg(OFF_GLOW, D_IN_PAD).astype(BF16)
        logits = jnp.dot(g_low, w2_ref[...], preferred_element_type=F32) + gb_ref[...]
        log_g = ((jnp.minimum(logits, 0.0) - jnp.log1p(jnp.exp(-jnp.abs(logits))))
                 * (LOG2E / GLA_GATE_NORMALIZER))
        rotary_piece(2, rq, rk, logits)
        yield
        t = seg(OFF_RG, OFF_GQ)
        rgate_ref[rows, :] = _silu(t)
        rotary_piece(3, rq, rk, t)
        yield
        gq_ref[rows, :] = seg(OFF_GQ, OFF_GK) * (GLA_KEY_DIM ** -0.5)
        gk_ref[rows, :] = seg(OFF_GK, OFF_GV)
        row = lax.broadcasted_iota(jnp.int32, (PROJ_SUB, GLA_KW), 0) & (GLA_CHUNK - 1)
        c = log_g
        step = 1
        while step < GLA_CHUNK:
            c = c + jnp.where(row >= step, pltpu.roll(c, step, 0), 0.0)
            step *= 2
        cum_ref[rows, :] = c
        yield
        ggate_ref[rows, :] = _silu(seg(OFF_GG, OFF_GLOW))

    _run_interleaved([sub_tile(r0) for r0 in range(0, x_ref.shape[0], PROJ_SUB)], stagger=1)


def _bcast_rows(parts):
    return jnp.concatenate([jnp.broadcast_to(p, (GLA_SUB, LANES)) for p in parts], axis=0)


def _mix_kernel(qr_ref, kr_ref, kd_ref, rv_ref, rgate_ref, gq_ref, gk_ref, cum_ref, gv_ref, ggate_ref,
                rtab_ref, rcdec_ref, rnw_ref, rnb_ref, gnw_ref,
                out_ref, rstate_ref, gstate_ref):
    @pl.when(pl.program_id(0) == 0)
    def _():
        rstate_ref[...] = jnp.zeros_like(rstate_ref)
        gstate_ref[...] = jnp.zeros_like(gstate_ref)

    c64, sub = GLA_CHUNK, GLA_SUB
    head0 = lax.broadcasted_iota(jnp.int32, (c64, LANES), 1) < GLA_KEY_DIM
    vhead0 = lax.broadcasted_iota(jnp.int32, (c64, 2 * GLA_VALUE_DIM), 1) < GLA_VALUE_DIM
    row_j = lax.broadcasted_iota(jnp.int32, (c64, LANES), 0)
    col_l = lax.broadcasted_iota(jnp.int32, (c64, LANES), 1) & (c64 - 1)
    sub_shift = sub.bit_length() - 1
    dist = (row_j >> sub_shift) - (col_l >> sub_shift)
    band = [dist == d for d in (1, 2, 3)]
    diag = (dist == 0) & (col_l <= row_j)
    zero_row = jnp.zeros((1, LANES), F32)

    def head_stack(t):
        z = jnp.zeros_like(t)
        return jnp.concatenate([jnp.where(head0, t, z), jnp.where(head0, z, t)], axis=0)

    def retention_chain(b, h, r0):
        rows = pl.ds(r0, RET_CHUNK)
        sl = slice(h * RET_DIM, (h + 1) * RET_DIM)
        q = qr_ref[b, rows, sl]
        k = kr_ref[b, rows, sl]
        v = rv_ref[b, rows, sl]
        state = rstate_ref[b, h]
        both = _dot_nt(q, jnp.concatenate([k, state.astype(BF16)], axis=0))
        upd = _dot_tn(v, kd_ref[b, rows, sl])
        yield
        scores = (both[:, 0:RET_CHUNK] * rtab_ref[h, 0]).astype(BF16)
        o = jnp.dot(scores, v, preferred_element_type=F32) + both[:, RET_CHUNK:] * rtab_ref[h, 1]
        rstate_ref[b, h] = state * rcdec_ref[h] + upd
        yield
        mu = jnp.mean(o, axis=-1, keepdims=True)
        yield
        d = o - mu
        var = jnp.mean(d * d, axis=-1, keepdims=True)
        yield
        y = (d * lax.rsqrt(var + EPS)) * rnw_ref[:, sl] + rnb_ref[:, sl]
        out_ref[b, rows, sl] = (y * rgate_ref[b, rows, sl]).astype(out_ref.dtype)

    def gla_chunk_front(b, p, r0):
        rows = pl.ds(r0, c64)
        ksl = slice(p * LANES, (p + 1) * LANES)
        vsl = slice(p * 2 * GLA_VALUE_DIM, (p + 1) * 2 * GLA_VALUE_DIM)
        q = gq_ref[b, rows, ksl]
        k = gk_ref[b, rows, ksl]
        c = cum_ref[b, rows, ksl]
        v = gv_ref[b, rows, vsl]
        ends = [c[(m + 1) * sub - 1:(m + 1) * sub, :] for m in range(c64 // sub)]
        last = ends[-1]
        ref_prev1 = _bcast_rows([zero_row, ends[0], ends[1], ends[2]])
        q1 = q * jnp.exp2(c - ref_prev1)
        q2 = q[2 * sub:] * jnp.exp2(c[2 * sub:] - _bcast_rows([ends[0], ends[1]]))
        q3 = q[3 * sub:] * jnp.exp2(c[3 * sub:] - _bcast_rows([ends[0]]))
        q_in = (q * jnp.exp2(c)).astype(BF16)
        k_band = (k * jnp.exp2(_bcast_rows(ends) - c)).astype(BF16)
        k_diag = (k * jnp.exp2(ref_prev1 - c)).astype(BF16)
        k_out = (k * jnp.exp2(last - c)).astype(BF16)
        lhs = jnp.concatenate([q1, q2, q3], axis=0).astype(BF16)
        rhs = jnp.concatenate([head_stack(k_band), head_stack(k_diag)], axis=0)
        prod = _dot_nt(lhs, rhs)
        upd = _dot_tn(jnp.concatenate([v[:, :GLA_VALUE_DIM], v[:, GLA_VALUE_DIM:]], axis=0), head_stack(k_out))
        zv = jnp.zeros_like(v)
        v_bd = jnp.concatenate([jnp.where(vhead0, v, zv), jnp.where(vhead0, zv, v)], axis=0)
        return dict(rows=rows, prod=prod, upd=upd, v_bd=v_bd, q_in=head_stack(q_in), decay=jnp.exp2(last))

    def gla_chain(b, p, r0):
        chunks = [gla_chunk_front(b, p, r0 + cc * c64) for cc in range(MIX_BLOCK // c64)]
        yield
        state = gstate_ref[b, p]
        outs = []
        for ch in chunks:
            prod = ch["prod"]
            near = jnp.where(diag, prod[0:c64, LANES:], jnp.where(band[0], prod[0:c64, 0:LANES], 0.0))
            far2 = jnp.where(band[1][2 * sub:], prod[c64:c64 + 2 * sub, 0:LANES], 0.0)
            far3 = jnp.where(band[2][3 * sub:], prod[c64 + 2 * sub:, 0:LANES], 0.0)
            attn = jnp.concatenate([near[:2 * sub], near[2 * sub:3 * sub] + far2[:sub],
                                    near[3 * sub:] + far2[sub:] + far3], axis=0).astype(BF16)
            inter = _dot_nt(ch["q_in"], state.astype(BF16))
            outs.append(jnp.dot(attn, ch["v_bd"], preferred_element_type=F32)
                        + jnp.concatenate([inter[:c64], inter[c64:]], axis=1))
            state = state * ch["decay"] + ch["upd"]
        gstate_ref[b, p] = state
        yield
        sums = [[jnp.mean(o[:, hh * GLA_VALUE_DIM:(hh + 1) * GLA_VALUE_DIM] ** 2, axis=-1, keepdims=True)
                 for hh in range(2)] for o in outs]
        yield
        for ch, o, ms in zip(chunks, outs, sums):
            for hh in range(2):
                gsl = slice((2 * p + hh) * GLA_VALUE_DIM, (2 * p + hh + 1) * GLA_VALUE_DIM)
                y = o[:, hh * GLA_VALUE_DIM:(hh + 1) * GLA_VALUE_DIM] * lax.rsqrt(ms[hh] + EPS)
                y = y * gnw_ref[:, gsl] * ggate_ref[b, ch["rows"], gsl]
                out_ref[b, ch["rows"], RET_W + gsl.start:RET_W + gsl.stop] = y.astype(out_ref.dtype)

    def delayed(chain, rounds):
        for _ in range(rounds):
            yield
        yield from chain

    for r0 in range(0, out_ref.shape[1], MIX_BLOCK):
        chains = []
        for b in range(out_ref.shape[0]):
            late = (b // MIX_GROUP) * MIX_DELAY
            chains += [delayed(retention_chain(b, h, r0), late) for h in range(RET_HEADS)]
            chains += [delayed(gla_chain(b, p, r0), late) for p in range(GLA_PAIRS)]
        _run_interleaved(chains)


def _ffn_kernel(x_ref, mixed_ref, wout_ref, n2w_ref, wup_ref, cw_ref, cb_ref, wdown_ref, fw_ref,
                y_ref, carry_ref, act_ref):
    d_ff = wdown_ref.shape[0]
    n_sub = x_ref.shape[0] // FFN_SUB

    @pl.when(pl.program_id(1) == 0)
    def _():
        carry_ref[...] = jnp.zeros_like(carry_ref)

    top = lax.broadcasted_iota(jnp.int32, (CARRY_ROWS, FFN_COLS), 0)
    tails = [dict() for _ in range(n_sub)]

    def sub_tile(i):
        rows = pl.ds(i * FFN_SUB, FFN_SUB)
        h = x_ref[rows, :] + jnp.dot(mixed_ref[rows, :], wout_ref[...], preferred_element_type=F32)
        ms = jnp.mean(h * h, axis=-1, keepdims=True)
        xn = (h * lax.rsqrt(ms + EPS) * n2w_ref[...]).astype(BF16)
        yield

        def conv_block(lo):
            cols = slice(lo, lo + FFN_COLS)
            up = jnp.dot(xn, wup_ref[:, cols], preferred_element_type=F32)
            prev = carry_ref[:, cols] if i == 0 else tails[i - 1][lo]
            tails[i][lo] = up[FFN_SUB - CARRY_ROWS:, :]
            if i == n_sub - 1:
                carry_ref[:, cols] = tails[i][lo]
            m1 = pltpu.roll(up, 1, 0)
            m2 = pltpu.roll(up, 2, 0)
            p1 = pltpu.roll(prev, 1, 0)
            p2 = pltpu.roll(prev, 2, 0)
            m1 = jnp.concatenate([jnp.where(top < 1, p1, m1[:CARRY_ROWS]), m1[CARRY_ROWS:]], axis=0)
            m2 = jnp.concatenate([jnp.where(top < 2, p2, m2[:CARRY_ROWS]), m2[CARRY_ROWS:]], axis=0)
            return (cw_ref[0:1, cols] * m2 + cw_ref[1:2, cols] * m1 + cw_ref[2:3, cols] * up) + cb_ref[:, cols]

        for j in range(d_ff // FFN_COLS):
            gate = conv_block(j * FFN_COLS)
            val = conv_block(d_ff + j * FFN_COLS)
            act_ref[rows, j * FFN_COLS:(j + 1) * FFN_COLS] = (_silu(gate) * val).astype(BF16)
            yield

        h2 = h + jnp.dot(act_ref[rows, :], wdown_ref[...], preferred_element_type=F32)
        ms2 = jnp.mean(h2 * h2, axis=-1, keepdims=True)
        y_ref[rows, :] = (h2 * lax.rsqrt(ms2 + EPS) * fw_ref[...]).astype(y_ref.dtype)

    _run_interleaved([sub_tile(i) for i in range(n_sub)], stagger=1)


def _retention_tables():
    c = RET_CHUNK
    log_gamma = np.log(1.0 - 2.0 ** (-5.0 - np.arange(RET_HEADS, dtype=np.float64)))
    j = np.arange(c, dtype=np.float64)
    diff = j[:, None] - j[None, :]
    intra = np.where(diff >= 0, np.exp(log_gamma[:, None, None] * np.maximum(diff, 0.0)), 0.0)
    q_dec = np.broadcast_to(np.exp(log_gamma[:, None] * (j + 1.0))[:, :, None], (RET_HEADS, c, c))
    tabs = np.stack([intra, q_dec], axis=1).astype(np.float32)
    chunk_dec = np.broadcast_to(np.exp(log_gamma * c)[:, None, None], (RET_HEADS, 1, c)).astype(np.float32)
    k_dec = np.exp(log_gamma[None, :] * (c - 1.0 - j)[:, None])
    k_dec = np.repeat(k_dec, RET_DIM, axis=1).astype(np.float32)
    return tabs, chunk_dec, k_dec


def _rotary_table():
    half = RET_DIM // 2
    inv_freq = ROPE_BASE ** (-jnp.arange(half, dtype=F32) / half)
    phase = jnp.concatenate([jnp.zeros((half,), F32), jnp.full((half,), math.pi / 2, F32)])
    return jnp.stack([jnp.concatenate([inv_freq, inv_freq]), phase])


def _const_spec(shape):
    zeros = (0,) * len(shape)
    return pl.BlockSpec(shape, lambda *_: zeros)


def _layer(h, positions, norm1_w, w_in, ret_norm_w, ret_norm_b, gla_gate_w2, gla_gate_b, gla_norm_w,
           w_out, norm2_w, ffn_w_up, ffn_conv_w, ffn_conv_b, ffn_w_down, final_w):
    b, s, d_model = h.shape
    t = b * s
    d_ff = ffn_w_down.shape[0]
    assert s % PROJ_ROWS == 0 and s % MIX_ROWS == 0 and s % FFN_ROWS == 0
    assert d_ff % FFN_COLS == 0 and w_in.shape == (d_model, D_IN_PROJ)

    x2 = h.reshape(t, d_model)
    pos2 = positions.reshape(t, 1)
    w2_p = jnp.pad(gla_gate_w2, ((0, LANES - GLA_RANK), (0, 0))).astype(BF16)
    row = lambda a: a.reshape(1, -1).astype(F32)
    rtab, rcdec, kdec = _retention_tables()

    def tok(width, rows):
        return pl.BlockSpec((rows, width), lambda i: (i, 0))

    widths = (RET_W, RET_W, RET_W, RET_W, RET_W, GLA_KW, GLA_KW, GLA_KW, GLA_VW, GLA_VW)
    dtypes = (BF16, BF16, BF16, BF16, F32, F32, F32, F32, BF16, F32)
    proj = pl.pallas_call(
        _proj_kernel,
        grid=(t // PROJ_ROWS,),
        in_specs=[tok(d_model, PROJ_ROWS), tok(1, PROJ_ROWS), _const_spec((1, d_model)),
                  pl.BlockSpec((d_model, D_IN_PROJ), lambda i: (0, 0), pipeline_mode=pl.Buffered(1)),
                  _const_spec((LANES, GLA_KW)), _const_spec((1, GLA_KW)),
                  _const_spec((2, LANES)), _const_spec(kdec.shape)],
        out_specs=[tok(w, PROJ_ROWS) for w in widths],
        out_shape=[jax.ShapeDtypeStruct((t, w), dt) for w, dt in zip(widths, dtypes)],
        scratch_shapes=[pltpu.VMEM((d_model, D_IN_PAD), BF16)],
        compiler_params=pltpu.CompilerParams(dimension_semantics=("arbitrary",), vmem_limit_bytes=VMEM_LIMIT),
        name="proj",
    )(x2, pos2, row(norm1_w), w_in.astype(F32), w2_p, row(gla_gate_b), _rotary_table(), jnp.asarray(kdec))

    def seq(width):
        return pl.BlockSpec((b, MIX_ROWS, width), lambda si: (0, si, 0))

    mixed = pl.pallas_call(
        _mix_kernel,
        grid=(s // MIX_ROWS,),
        in_specs=[seq(w) for w in widths]
        + [_const_spec(rtab.shape), _const_spec(rcdec.shape),
           _const_spec((1, RET_W)), _const_spec((1, RET_W)), _const_spec((1, GLA_VW))],
        out_specs=seq(RET_W + GLA_VW),
        out_shape=jax.ShapeDtypeStruct((b, s, RET_W + GLA_VW), BF16),
        scratch_shapes=[pltpu.VMEM((b, RET_HEADS, RET_DIM, RET_DIM), F32),
                        pltpu.VMEM((b, GLA_PAIRS, GLA_VALUE_DIM, LANES), F32)],
        compiler_params=pltpu.CompilerParams(dimension_semantics=("arbitrary",),
                                             vmem_limit_bytes=VMEM_LIMIT),
        name="mix",
    )(*[a.reshape(b, s, -1) for a in proj], jnp.asarray(rtab), jnp.asarray(rcdec),
      row(ret_norm_w), row(ret_norm_b), row(gla_norm_w))
    mixed = mixed.reshape(t, RET_W + GLA_VW)

    fsteps = s // FFN_ROWS

    def ftok(width):
        return pl.BlockSpec((FFN_ROWS, width), lambda bi, si: (bi * fsteps + si, 0))

    def weight(shape):
        zeros = (0,) * len(shape)
        return pl.BlockSpec(shape, lambda *_: zeros, pipeline_mode=pl.Buffered(1))

    y = pl.pallas_call(
        _ffn_kernel,
        grid=(b, fsteps),
        in_specs=[ftok(d_model), ftok(d_model), weight((d_model, d_model)), _const_spec((1, d_model)),
                  weight((d_model, 2 * d_ff)), _const_spec((CONV_WIDTH, 2 * d_ff)), _const_spec((1, 2 * d_ff)),
                  weight((d_ff, d_model)), _const_spec((1, d_model))],
        out_specs=ftok(d_model),
        out_shape=jax.ShapeDtypeStruct((t, d_model), h.dtype),
        scratch_shapes=[pltpu.VMEM((CARRY_ROWS, 2 * d_ff), F32), pltpu.VMEM((FFN_ROWS, d_ff), BF16)],
        compiler_params=pltpu.CompilerParams(dimension_semantics=("parallel", "arbitrary"),
                                             vmem_limit_bytes=VMEM_LIMIT),
        name="ffn",
    )(x2, mixed, w_out.astype(BF16), row(norm2_w), ffn_w_up.astype(BF16), ffn_conv_w.astype(F32),
      row(ffn_conv_b), ffn_w_down.astype(BF16), row(final_w))
    return y.reshape(b, s, d_model)


def kernel(x, positions, norm1_w, w_in, ret_norm_w, ret_norm_b, gla_gate_w2, gla_gate_b, gla_norm_w, w_out,
           norm2_w, ffn_w_up, ffn_conv_w, ffn_conv_b, ffn_w_down, final_norm_w):
    depth = w_in.shape[0]
    assert depth == 1, "the final RMSNorm is fused into the (single) layer's ffn kernel"
    return _layer(x, positions, norm1_w[0], w_in[0], ret_norm_w[0], ret_norm_b[0], gla_gate_w2[0], gla_gate_b[0],
                  gla_norm_w[0], w_out[0], norm2_w[0], ffn_w_up[0], ffn_conv_w[0], ffn_conv_b[0], ffn_w_down[0],
                  final_norm_w)
```

```python
import math

import numpy as np
import jax
import jax.numpy as jnp
from jax import lax
from jax.experimental import pallas as pl
from jax.experimental.pallas import tpu as pltpu

F32 = jnp.float32
BF16 = jnp.bfloat16

LANES = 128
SUBLANES = 8
EPS = 1e-6
ROPE_BASE = 10000.0
LOG2E = math.log2(math.e)

RET_HEADS = 4
RET_DIM = 128
RET_W = RET_HEADS * RET_DIM
RET_CHUNK = 128
GLA_HEADS = 4
GLA_KEY_DIM = 64
GLA_VALUE_DIM = 128
GLA_KW = GLA_HEADS * GLA_KEY_DIM
GLA_VW = GLA_HEADS * GLA_VALUE_DIM
GLA_PAIRS = GLA_HEADS // 2
GLA_RANK = 16
GLA_GATE_NORMALIZER = 16.0
GLA_CHUNK = 64
GLA_SUB = 16
CONV_WIDTH = 3

OFF_RQ, OFF_RK, OFF_RV, OFF_RG = 0, 512, 1024, 1536
OFF_GQ, OFF_GK, OFF_GV, OFF_GG, OFF_GLOW = 2048, 2304, 2560, 3072, 3584
D_IN_PROJ = OFF_GLOW + GLA_RANK
D_IN_PAD = OFF_GLOW + LANES

SEQ_BLOCK = 128
PROJ_BATCHES = 2
PROJ_PIECE = 64
PROJ_CAST_COLS = 512
MIX_DELAY = 0
FFN_ROWS = 512
FFN_SUB = 256
FFN_COLS = 256
CARRY_ROWS = 8

VMEM_LIMIT = 56 * 1024 * 1024

STAGE_NAMES = ("qr", "kr", "kd", "rv", "rgate", "gq", "gk", "cum", "gv", "ggate")
STAGE_WIDTHS = (RET_W, RET_W, RET_W, RET_W, RET_W, GLA_KW, GLA_KW, GLA_KW, GLA_VW, GLA_VW)
STAGE_DTYPES = (BF16, BF16, BF16, BF16, F32, F32, F32, F32, BF16, F32)


def _silu(v):
    return v * (1.0 / (1.0 + jnp.exp(-v)))


def _dot_nt(a, b):
    return lax.dot_general(a, b, (((1,), (1,)), ((), ())), preferred_element_type=F32)


def _dot_tn(a, b):
    return lax.dot_general(a, b, (((0,), (0,)), ((), ())), preferred_element_type=F32)


def _delayed(chain, rounds):
    for _ in range(rounds):
        yield
    yield from chain


def _run_interleaved(chains):
    while chains:
        chains = [c for c in chains if next(c, StopIteration) is not StopIteration]


def _zero_after(value):
    bits = lax.bitcast_convert_type(value[0:SUBLANES, 0:LANES], jnp.uint32)
    half = jnp.uint32(16)
    return lax.shift_right_logical(lax.shift_right_logical(bits, half), half).astype(F32)[0:1, :]


def _bcast_rows(parts):
    return jnp.concatenate([jnp.broadcast_to(p, (GLA_SUB, LANES)) for p in parts], axis=0)


def _proj_chain(g, x_ref, pos_ref, n1w_ref, wb_ref, w2_ref, gb_ref, freq_ref, kdec_ref, st):
    b0 = g * PROJ_BATCHES
    rows = PROJ_BATCHES * SEQ_BLOCK
    x = jnp.concatenate([x_ref[b0 + bb] for bb in range(PROJ_BATCHES)], axis=0)
    ms = jnp.mean(x * x, axis=-1, keepdims=True)
    xn = (x * lax.rsqrt(ms + EPS) * n1w_ref[...]).astype(BF16)

    def seg(lo, hi):
        return jnp.dot(xn, wb_ref[:, lo:hi], preferred_element_type=F32)

    def put(name, val):
        for bb in range(PROJ_BATCHES):
            st[name][b0 + bb] = val[bb * SEQ_BLOCK:(bb + 1) * SEQ_BLOCK].astype(st[name].dtype)

    low = lax.broadcasted_iota(jnp.int32, (PROJ_PIECE, LANES), 1) < RET_DIM // 2
    per_block = SEQ_BLOCK // PROJ_PIECE

    def rotary_piece(i, rq, rk, after):
        bb, part = divmod(i, per_block)
        prow = slice(part * PROJ_PIECE, (part + 1) * PROJ_PIECE)
        ang = (pos_ref[b0 + bb, prow, :].astype(F32) * freq_ref[0:1, :] - freq_ref[1:2, :]
               + _zero_after(after))
        cs = jnp.cos(ang)
        sc = pltpu.roll(cs, RET_DIM // 2, 1)
        cos_full = jnp.where(low, cs, sc)
        sin_signed = jnp.where(low, -sc, cs)
        for h in range(RET_HEADS):
            sl = slice(h * RET_DIM, (h + 1) * RET_DIM)
            tq = rq[i * PROJ_PIECE:(i + 1) * PROJ_PIECE, sl]
            tk = rk[i * PROJ_PIECE:(i + 1) * PROJ_PIECE, sl]
            st["qr"][b0 + bb, prow, sl] = (tq * cos_full
                                           + pltpu.roll(tq, RET_DIM // 2, 1) * sin_signed).astype(BF16)
            k_rot = (tk * cos_full + pltpu.roll(tk, RET_DIM // 2, 1) * sin_signed) * (RET_DIM ** -0.5)
            st["kr"][b0 + bb, prow, sl] = k_rot.astype(BF16)
            st["kd"][b0 + bb, prow, sl] = (k_rot * kdec_ref[prow, sl]).astype(BF16)

    yield
    rq = seg(OFF_RQ, OFF_RK)
    yield
    rk = seg(OFF_RK, OFF_RV)
    yield
    t = seg(OFF_RV, OFF_RG)
    put("rv", t)
    rotary_piece(0, rq, rk, t)
    yield
    t = seg(OFF_GV, OFF_GG)
    put("gv", t)
    rotary_piece(1, rq, rk, t)
    yield
    g_low = seg(OFF_GLOW, D_IN_PAD).astype(BF16)
    logits = jnp.dot(g_low, w2_ref[...], preferred_element_type=F32) + gb_ref[...]
    log_g = ((jnp.minimum(logits, 0.0) - jnp.log1p(jnp.exp(-jnp.abs(logits))))
             * (LOG2E / GLA_GATE_NORMALIZER))
    rotary_piece(2, rq, rk, logits)
    yield
    t = seg(OFF_RG, OFF_GQ)
    put("rgate", _silu(t))
    rotary_piece(3, rq, rk, t)
    yield
    put("gq", seg(OFF_GQ, OFF_GK) * (GLA_KEY_DIM ** -0.5))
    put("gk", seg(OFF_GK, OFF_GV))
    row = lax.broadcasted_iota(jnp.int32, (rows, GLA_KW), 0) & (GLA_CHUNK - 1)
    c = log_g
    step = 1
    while step < GLA_CHUNK:
        c = c + jnp.where(row >= step, pltpu.roll(c, step, 0), 0.0)
        step *= 2
    put("cum", c)
    yield
    put("ggate", _silu(seg(OFF_GG, OFF_GLOW)))


def _mix_chains(st, rtab_ref, rcdec_ref, rnw_ref, rnb_ref, gnw_ref, out_ref, rstate_ref, gstate_ref):
    c64, sub = GLA_CHUNK, GLA_SUB
    head0 = lax.broadcasted_iota(jnp.int32, (c64, LANES), 1) < GLA_KEY_DIM
    vhead0 = lax.broadcasted_iota(jnp.int32, (c64, 2 * GLA_VALUE_DIM), 1) < GLA_VALUE_DIM
    row_j = lax.broadcasted_iota(jnp.int32, (c64, LANES), 0)
    col_l = lax.broadcasted_iota(jnp.int32, (c64, LANES), 1) & (c64 - 1)
    sub_shift = sub.bit_length() - 1
    dist = (row_j >> sub_shift) - (col_l >> sub_shift)
    band = [dist == d for d in (1, 2, 3)]
    diag = (dist == 0) & (col_l <= row_j)
    zero_row = jnp.zeros((1, LANES), F32)

    def head_stack(t):
        z = jnp.zeros_like(t)
        return jnp.concatenate([jnp.where(head0, t, z), jnp.where(head0, z, t)], axis=0)

    def retention_chain(b, h):
        sl = slice(h * RET_DIM, (h + 1) * RET_DIM)
        q = st["qr"][b, :, sl]
        k = st["kr"][b, :, sl]
        v = st["rv"][b, :, sl]
        state = rstate_ref[b, h]
        both = _dot_nt(q, jnp.concatenate([k, state.astype(BF16)], axis=0))
        upd = _dot_tn(v, st["kd"][b, :, sl])
        yield
        scores = (both[:, 0:RET_CHUNK] * rtab_ref[h, 0]).astype(BF16)
        o = jnp.dot(scores, v, preferred_element_type=F32) + both[:, RET_CHUNK:] * rtab_ref[h, 1]
        rstate_ref[b, h] = state * rcdec_ref[h] + upd
        yield
        mu = jnp.mean(o, axis=-1, keepdims=True)
        yield
        d = o - mu
        var = jnp.mean(d * d, axis=-1, keepdims=True)
        yield
        y = (d * lax.rsqrt(var + EPS)) * rnw_ref[:, sl] + rnb_ref[:, sl]
        out_ref[b, :, sl] = (y * st["rgate"][b, :, sl]).astype(out_ref.dtype)

    def gla_chunk_front(b, p, r0):
        rows = slice(r0, r0 + c64)
        ksl = slice(p * LANES, (p + 1) * LANES)
        vsl = slice(p * 2 * GLA_VALUE_DIM, (p + 1) * 2 * GLA_VALUE_DIM)
        q = st["gq"][b, rows, ksl]
        k = st["gk"][b, rows, ksl]
        c = st["cum"][b, rows, ksl]
        v = st["gv"][b, rows, vsl]
        ends = [c[(m + 1) * sub - 1:(m + 1) * sub, :] for m in range(c64 // sub)]
        last = ends[-1]
        ref_prev1 = _bcast_rows([zero_row, ends[0], ends[1], ends[2]])
        q1 = q * jnp.exp2(c - ref_prev1)
        q2 = q[2 * sub:] * jnp.exp2(c[2 * sub:] - _bcast_rows([ends[0], ends[1]]))
        q3 = q[3 * sub:] * jnp.exp2(c[3 * sub:] - _bcast_rows([ends[0]]))
        q_in = (q * jnp.exp2(c)).astype(BF16)
        k_band = (k * jnp.exp2(_bcast_rows(ends) - c)).astype(BF16)
        k_diag = (k * jnp.exp2(ref_prev1 - c)).astype(BF16)
        k_out = (k * jnp.exp2(last - c)).astype(BF16)
        lhs = jnp.concatenate([q1, q2, q3], axis=0).astype(BF16)
        rhs = jnp.concatenate([head_stack(k_band), head_stack(k_diag)], axis=0)
        prod = _dot_nt(lhs, rhs)
        upd = _dot_tn(jnp.concatenate([v[:, :GLA_VALUE_DIM], v[:, GLA_VALUE_DIM:]], axis=0), head_stack(k_out))
        zv = jnp.zeros_like(v)
        v_bd = jnp.concatenate([jnp.where(vhead0, v, zv), jnp.where(vhead0, zv, v)], axis=0)
        return dict(rows=rows, prod=prod, upd=upd, v_bd=v_bd, q_in=head_stack(q_in), decay=jnp.exp2(last))

    def gla_chain(b, p):
        chunks = [gla_chunk_front(b, p, cc * c64) for cc in range(SEQ_BLOCK // c64)]
        yield
        state = gstate_ref[b, p]
        outs = []
        for ch in chunks:
            prod = ch["prod"]
            near = jnp.where(diag, prod[0:c64, LANES:], jnp.where(band[0], prod[0:c64, 0:LANES], 0.0))
            far2 = jnp.where(band[1][2 * sub:], prod[c64:c64 + 2 * sub, 0:LANES], 0.0)
            far3 = jnp.where(band[2][3 * sub:], prod[c64 + 2 * sub:, 0:LANES], 0.0)
            attn = jnp.concatenate([near[:2 * sub], near[2 * sub:3 * sub] + far2[:sub],
                                    near[3 * sub:] + far2[sub:] + far3], axis=0).astype(BF16)
            inter = _dot_nt(ch["q_in"], state.astype(BF16))
            outs.append(jnp.dot(attn, ch["v_bd"], preferred_element_type=F32)
                        + jnp.concatenate([inter[:c64], inter[c64:]], axis=1))
            state = state * ch["decay"] + ch["upd"]
        gstate_ref[b, p] = state
        yield
        sums = [[jnp.mean(o[:, hh * GLA_VALUE_DIM:(hh + 1) * GLA_VALUE_DIM] ** 2, axis=-1, keepdims=True)
                 for hh in range(2)] for o in outs]
        yield
        for ch, o, ms in zip(chunks, outs, sums):
            for hh in range(2):
                gsl = slice((2 * p + hh) * GLA_VALUE_DIM, (2 * p + hh + 1) * GLA_VALUE_DIM)
                y = o[:, hh * GLA_VALUE_DIM:(hh + 1) * GLA_VALUE_DIM] * lax.rsqrt(ms[hh] + EPS)
                y = y * gnw_ref[:, gsl] * st["ggate"][b, ch["rows"], gsl]
                out_ref[b, ch["rows"], RET_W + gsl.start:RET_W + gsl.stop] = y.astype(out_ref.dtype)

    chains = []
    for b in range(out_ref.shape[0]):
        late = b * MIX_DELAY
        chains += [_delayed(retention_chain(b, h), late) for h in range(RET_HEADS)]
        chains += [_delayed(gla_chain(b, p), late) for p in range(GLA_PAIRS)]
    return chains


def _projmix_kernel(x_ref, pos_ref, n1w_ref, w_ref, w2_ref, gb_ref, freq_ref, kdec_ref,
                    rtab_ref, rcdec_ref, rnw_ref, rnb_ref, gnw_ref,
                    out_ref, wb_ref, rstate_ref, gstate_ref, *stage_refs):
    n = len(STAGE_NAMES)
    stages = [dict(zip(STAGE_NAMES, stage_refs[:n])), dict(zip(STAGE_NAMES, stage_refs[n:]))]
    step = pl.program_id(0)

    @pl.when(step == 0)
    def _():
        for lo in range(0, OFF_GLOW, PROJ_CAST_COLS):
            wb_ref[:, lo:lo + PROJ_CAST_COLS] = w_ref[:, lo:lo + PROJ_CAST_COLS].astype(BF16)
        wb_ref[:, OFF_GLOW:] = jnp.zeros((wb_ref.shape[0], LANES), BF16)
        wb_ref[:, OFF_GLOW:D_IN_PROJ] = w_ref[:, OFF_GLOW:].astype(BF16)
        rstate_ref[...] = jnp.zeros_like(rstate_ref)
        gstate_ref[...] = jnp.zeros_like(gstate_ref)
        for ref in stages[1].values():
            ref[...] = jnp.zeros_like(ref)

    def body(write_stage, read_stage):
        groups = x_ref.shape[0] // PROJ_BATCHES
        chains = [_delayed(_proj_chain(g, x_ref, pos_ref, n1w_ref, wb_ref, w2_ref, gb_ref, freq_ref, kdec_ref,
                                       write_stage), g) for g in range(groups)]
        chains += _mix_chains(read_stage, rtab_ref, rcdec_ref, rnw_ref, rnb_ref, gnw_ref,
                              out_ref, rstate_ref, gstate_ref)
        _run_interleaved(chains)

    @pl.when(step % 2 == 0)
    def _():
        body(stages[0], stages[1])

    @pl.when(step % 2 == 1)
    def _():
        body(stages[1], stages[0])


def _ffn_kernel(x_ref, mixed_ref, wout_ref, n2w_ref, wup_ref, cw_ref, cb_ref, wdown_ref, fw_ref,
                y_ref, carry_ref, act_ref):
    d_ff = wdown_ref.shape[0]
    n_sub = x_ref.shape[0] // FFN_SUB

    @pl.when(pl.program_id(1) == 0)
    def _():
        carry_ref[...] = jnp.zeros_like(carry_ref)

    top = lax.broadcasted_iota(jnp.int32, (CARRY_ROWS, FFN_COLS), 0)
    tails = [dict() for _ in range(n_sub)]

    def sub_tile(i):
        rows = pl.ds(i * FFN_SUB, FFN_SUB)
        h = x_ref[rows, :] + jnp.dot(mixed_ref[rows, :], wout_ref[...], preferred_element_type=F32)
        ms = jnp.mean(h * h, axis=-1, keepdims=True)
        xn = (h * lax.rsqrt(ms + EPS) * n2w_ref[...]).astype(BF16)
        yield

        def conv_block(lo):
            cols = slice(lo, lo + FFN_COLS)
            up = jnp.dot(xn, wup_ref[:, cols], preferred_element_type=F32)
            prev = carry_ref[:, cols] if i == 0 else tails[i - 1][lo]
            tails[i][lo] = up[FFN_SUB - CARRY_ROWS:, :]
            if i == n_sub - 1:
                carry_ref[:, cols] = tails[i][lo]
            m1 = pltpu.roll(up, 1, 0)
            m2 = pltpu.roll(up, 2, 0)
            p1 = pltpu.roll(prev, 1, 0)
            p2 = pltpu.roll(prev, 2, 0)
            m1 = jnp.concatenate([jnp.where(top < 1, p1, m1[:CARRY_ROWS]), m1[CARRY_ROWS:]], axis=0)
            m2 = jnp.concatenate([jnp.where(top < 2, p2, m2[:CARRY_ROWS]), m2[CARRY_ROWS:]], axis=0)
            return (cw_ref[0:1, cols] * m2 + cw_ref[1:2, cols] * m1 + cw_ref[2:3, cols] * up) + cb_ref[:, cols]

        for j in range(d_ff // FFN_COLS):
            gate = conv_block(j * FFN_COLS)
            val = conv_block(d_ff + j * FFN_COLS)
            act_ref[rows, j * FFN_COLS:(j + 1) * FFN_COLS] = (_silu(gate) * val).astype(BF16)
            yield

        h2 = h + jnp.dot(act_ref[rows, :], wdown_ref[...], preferred_element_type=F32)
        ms2 = jnp.mean(h2 * h2, axis=-1, keepdims=True)
        y_ref[rows, :] = (h2 * lax.rsqrt(ms2 + EPS) * fw_ref[...]).astype(y_ref.dtype)

    _run_interleaved([_delayed(sub_tile(i), i) for i in range(n_sub)])


def _retention_tables():
    c = RET_CHUNK
    log_gamma = np.log(1.0 - 2.0 ** (-5.0 - np.arange(RET_HEADS, dtype=np.float64)))
    j = np.arange(c, dtype=np.float64)
    diff = j[:, None] - j[None, :]
    intra = np.where(diff >= 0, np.exp(log_gamma[:, None, None] * np.maximum(diff, 0.0)), 0.0)
    q_dec = np.broadcast_to(np.exp(log_gamma[:, None] * (j + 1.0))[:, :, None], (RET_HEADS, c, c))
    tabs = np.stack([intra, q_dec], axis=1).astype(np.float32)
    chunk_dec = np.broadcast_to(np.exp(log_gamma * c)[:, None, None], (RET_HEADS, 1, c)).astype(np.float32)
    k_dec = np.exp(log_gamma[None, :] * (c - 1.0 - j)[:, None])
    k_dec = np.repeat(k_dec, RET_DIM, axis=1).astype(np.float32)
    return tabs, chunk_dec, k_dec


def _rotary_table():
    half = RET_DIM // 2
    inv_freq = ROPE_BASE ** (-jnp.arange(half, dtype=F32) / half)
    phase = jnp.concatenate([jnp.zeros((half,), F32), jnp.full((half,), math.pi / 2, F32)])
    return jnp.stack([jnp.concatenate([inv_freq, inv_freq]), phase])


def _const_spec(shape):
    zeros = (0,) * len(shape)
    return pl.BlockSpec(shape, lambda *_: zeros)


def _layer(layer, h, positions, norm1_w, w_in, ret_norm_w, ret_norm_b, gla_gate_w2, gla_gate_b, gla_norm_w,
           w_out, norm2_w, ffn_w_up, ffn_conv_w, ffn_conv_b, ffn_w_down, final_w):
    b, s, d_model = h.shape
    t = b * s
    d_ff = ffn_w_down.shape[1]
    assert s % SEQ_BLOCK == 0 and s % FFN_ROWS == 0 and b % PROJ_BATCHES == 0
    assert SEQ_BLOCK == RET_CHUNK and d_ff % FFN_COLS == 0 and w_in.shape[1:] == (d_model, D_IN_PROJ)

    w2_p = jnp.pad(gla_gate_w2[layer], ((0, LANES - GLA_RANK), (0, 0))).astype(BF16)
    row = lambda a: a.reshape(1, -1).astype(F32)
    rtab, rcdec, kdec = _retention_tables()
    n_blocks = s // SEQ_BLOCK

    def ahead(width):
        return pl.BlockSpec((b, SEQ_BLOCK, width), lambda i: (0, jnp.minimum(i, n_blocks - 1), 0))

    mixed = pl.pallas_call(
        _projmix_kernel,
        grid=(n_blocks + 1,),
        in_specs=[ahead(d_model), ahead(1), _const_spec((1, d_model)),
                  pl.BlockSpec((None, d_model, D_IN_PROJ), lambda i: (layer, 0, 0), pipeline_mode=pl.Buffered(1)),
                  _const_spec((LANES, GLA_KW)), _const_spec((1, GLA_KW)), _const_spec((2, LANES)),
                  _const_spec(kdec.shape), _const_spec(rtab.shape), _const_spec(rcdec.shape),
                  _const_spec((1, RET_W)), _const_spec((1, RET_W)), _const_spec((1, GLA_VW))],
        out_specs=pl.BlockSpec((b, SEQ_BLOCK, RET_W + GLA_VW), lambda i: (0, jnp.maximum(i - 1, 0), 0)),
        out_shape=jax.ShapeDtypeStruct((b, s, RET_W + GLA_VW), BF16),
        scratch_shapes=[pltpu.VMEM((d_model, D_IN_PAD), BF16),
                        pltpu.VMEM((b, RET_HEADS, RET_DIM, RET_DIM), F32),
                        pltpu.VMEM((b, GLA_PAIRS, GLA_VALUE_DIM, LANES), F32)]
        + 2 * [pltpu.VMEM((b, SEQ_BLOCK, w), dt) for w, dt in zip(STAGE_WIDTHS, STAGE_DTYPES)],
        compiler_params=pltpu.CompilerParams(dimension_semantics=("arbitrary",), vmem_limit_bytes=VMEM_LIMIT),
        name="projmix",
    )(h, positions.reshape(b, s, 1), row(norm1_w[layer]), w_in.astype(F32), w2_p, row(gla_gate_b[layer]),
      _rotary_table(), jnp.asarray(kdec), jnp.asarray(rtab), jnp.asarray(rcdec),
      row(ret_norm_w[layer]), row(ret_norm_b[layer]), row(gla_norm_w[layer]))
    mixed = mixed.reshape(t, RET_W + GLA_VW)

    fsteps = s // FFN_ROWS

    def ftok(width):
        return pl.BlockSpec((FFN_ROWS, width), lambda bi, si: (bi * fsteps + si, 0))

    def weight(shape):
        zeros = (0,) * len(shape)
        return pl.BlockSpec(shape, lambda *_: zeros, pipeline_mode=pl.Buffered(1))

    y = pl.pallas_call(
        _ffn_kernel,
        grid=(b, fsteps),
        in_specs=[ftok(d_model), ftok(d_model), weight((d_model, d_model)), _const_spec((1, d_model)),
                  weight((d_model, 2 * d_ff)), _const_spec((CONV_WIDTH, 2 * d_ff)), _const_spec((1, 2 * d_ff)),
                  weight((d_ff, d_model)), _const_spec((1, d_model))],
        out_specs=ftok(d_model),
        out_shape=jax.ShapeDtypeStruct((t, d_model), h.dtype),
        scratch_shapes=[pltpu.VMEM((CARRY_ROWS, 2 * d_ff), F32), pltpu.VMEM((FFN_ROWS, d_ff), BF16)],
        compiler_params=pltpu.CompilerParams(dimension_semantics=("parallel", "arbitrary"),
                                             vmem_limit_bytes=VMEM_LIMIT),
        name="ffn",
    )(h.reshape(t, d_model), mixed, w_out[layer].astype(BF16), row(norm2_w[layer]), ffn_w_up[layer].astype(BF16),
      ffn_conv_w[layer].astype(F32), row(ffn_conv_b[layer]), ffn_w_down[layer].astype(BF16), row(final_w))
    return y.reshape(b, s, d_model)


def kernel(x, positions, norm1_w, w_in, ret_norm_w, ret_norm_b, gla_gate_w2, gla_gate_b, gla_norm_w, w_out,
           norm2_w, ffn_w_up, ffn_conv_w, ffn_conv_b, ffn_w_down, final_norm_w):
    depth = w_in.shape[0]
    assert depth == 1, "the final RMSNorm is fused into the (single) layer's ffn kernel"
    return _layer(0, x, positions, norm1_w, w_in, ret_norm_w, ret_norm_b, gla_gate_w2, gla_gate_b, gla_norm_w,
                  w_out, norm2_w, ffn_w_up, ffn_conv_w, ffn_conv_b, ffn_w_down, final_norm_w)
```

```python
import math

import numpy as np
import jax
import jax.numpy as jnp
from jax import lax
from jax.experimental import pallas as pl
from jax.experimental.pallas import tpu as pltpu

F32 = jnp.float32
BF16 = jnp.bfloat16

LANES = 128
SUBLANES = 8
EPS = 1e-6
ROPE_BASE = 10000.0
LOG2E = math.log2(math.e)

RET_HEADS = 4
RET_DIM = 128
RET_W = RET_HEADS * RET_DIM
RET_CHUNK = 128
GLA_HEADS = 4
GLA_KEY_DIM = 64
GLA_VALUE_DIM = 128
GLA_KW = GLA_HEADS * GLA_KEY_DIM
GLA_VW = GLA_HEADS * GLA_VALUE_DIM
GLA_PAIRS = GLA_HEADS // 2
GLA_RANK = 16
GLA_GATE_NORMALIZER = 16.0
GLA_CHUNK = 64
GLA_SUB = 16
CONV_WIDTH = 3

OFF_RQ, OFF_RK, OFF_RV, OFF_RG = 0, 512, 1024, 1536
OFF_GQ, OFF_GK, OFF_GV, OFF_GG, OFF_GLOW = 2048, 2304, 2560, 3072, 3584
D_IN_PROJ = OFF_GLOW + GLA_RANK
D_IN_PAD = OFF_GLOW + LANES

SEQ_BLOCK = 128
PROJ_BATCHES = 2
PROJ_PIECE = 64
PROJ_CAST_COLS = 512
MIX_DELAY = 0
FFN_ROWS = 512
FFN_SUB = 256
FFN_COLS = 256
CARRY_ROWS = SUBLANES

VMEM_LIMIT = 56 * 1024 * 1024

STAGE_NAMES = ("qr", "kr", "kd", "rv", "rgate", "gq", "gk", "cum", "gv", "ggate")
STAGE_WIDTHS = (RET_W, RET_W, RET_W, RET_W, RET_W, GLA_KW, GLA_KW, GLA_KW, GLA_VW, GLA_VW)
STAGE_DTYPES = (BF16, BF16, BF16, BF16, F32, F32, F32, F32, BF16, F32)


def _silu(v):
    return v * (1.0 / (1.0 + jnp.exp(-v)))


def _dot_nt(a, b):
    return lax.dot_general(a, b, (((1,), (1,)), ((), ())), preferred_element_type=F32)


def _dot_tn(a, b):
    return lax.dot_general(a, b, (((0,), (0,)), ((), ())), preferred_element_type=F32)


def _delayed(chain, rounds):
    for _ in range(rounds):
        yield
    yield from chain


def _run_interleaved(chains):
    while chains:
        chains = [c for c in chains if next(c, StopIteration) is not StopIteration]


def _zero_after(value):
    bits = lax.bitcast_convert_type(value[0:SUBLANES, 0:LANES], jnp.uint32)
    half = jnp.uint32(16)
    return lax.shift_right_logical(lax.shift_right_logical(bits, half), half).astype(F32)[0:1, :]


_HALF_PI_PARTS = (1.5703125, 4.837512969970703125e-4, 7.54978995489188216e-8)
_SIN_POLY = (-1.6666654611e-1, 8.3321608736e-3, -1.9515295891e-4)
_COS_POLY = (4.166664568298827e-2, -1.388731625493765e-3, 2.443315711809948e-5)


def _cos_rotary(x):
    quad = jnp.round(x * (2.0 / math.pi))
    r = ((x - quad * _HALF_PI_PARTS[0]) - quad * _HALF_PI_PARTS[1]) - quad * _HALF_PI_PARTS[2]
    r2 = r * r
    sin_r = r + r * r2 * (_SIN_POLY[0] + r2 * (_SIN_POLY[1] + r2 * _SIN_POLY[2]))
    cos_r = 1.0 - 0.5 * r2 + r2 * r2 * (_COS_POLY[0] + r2 * (_COS_POLY[1] + r2 * _COS_POLY[2]))
    q = quad.astype(jnp.int32)
    swap = (q & 1) == 1
    negate = ((q + 1) & 2) == 2
    val = jnp.where(swap, sin_r, cos_r)
    return jnp.where(negate, -val, val)


def _bcast_rows(parts):
    return jnp.concatenate([jnp.broadcast_to(p, (GLA_SUB, LANES)) for p in parts], axis=0)


def _proj_chain(g, x_ref, pos_ref, n1w_ref, wb_ref, w2_ref, gb_ref, freq_ref, kdec_ref, st):
    b0 = g * PROJ_BATCHES
    rows = PROJ_BATCHES * SEQ_BLOCK
    x = jnp.concatenate([x_ref[b0 + bb] for bb in range(PROJ_BATCHES)], axis=0)
    ms = jnp.mean(x * x, axis=-1, keepdims=True)
    xn = (x * lax.rsqrt(ms + EPS) * n1w_ref[...]).astype(BF16)

    def seg(lo, hi):
        return jnp.dot(xn, wb_ref[:, lo:hi], preferred_element_type=F32)

    def put(name, val):
        for bb in range(PROJ_BATCHES):
            st[name][b0 + bb] = val[bb * SEQ_BLOCK:(bb + 1) * SEQ_BLOCK].astype(st[name].dtype)

    low = lax.broadcasted_iota(jnp.int32, (PROJ_PIECE, LANES), 1) < RET_DIM // 2
    per_block = SEQ_BLOCK // PROJ_PIECE

    def rotary_piece(i, rq, rk, after):
        bb, part = divmod(i, per_block)
        prow = slice(part * PROJ_PIECE, (part + 1) * PROJ_PIECE)
        ang = (pos_ref[b0 + bb, prow, :].astype(F32) * freq_ref[0:1, :] - freq_ref[1:2, :]
               + _zero_after(after))
        cs = _cos_rotary(ang)
        sc = pltpu.roll(cs, RET_DIM // 2, 1)
        cos_full = jnp.where(low, cs, sc)
        sin_signed = jnp.where(low, -sc, cs)
        for h in range(RET_HEADS):
            sl = slice(h * RET_DIM, (h + 1) * RET_DIM)
            tq = rq[i * PROJ_PIECE:(i + 1) * PROJ_PIECE, sl]
            tk = rk[i * PROJ_PIECE:(i + 1) * PROJ_PIECE, sl]
            st["qr"][b0 + bb, prow, sl] = (tq * cos_full
                                           + pltpu.roll(tq, RET_DIM // 2, 1) * sin_signed).astype(BF16)
            k_rot = (tk * cos_full + pltpu.roll(tk, RET_DIM // 2, 1) * sin_signed) * (RET_DIM ** -0.5)
            st["kr"][b0 + bb, prow, sl] = k_rot.astype(BF16)
            st["kd"][b0 + bb, prow, sl] = (k_rot * kdec_ref[prow, sl]).astype(BF16)

    yield
    rq = seg(OFF_RQ, OFF_RK)
    yield
    rk = seg(OFF_RK, OFF_RV)
    yield
    t = seg(OFF_RV, OFF_RG)
    put("rv", t)
    rotary_piece(0, rq, rk, t)
    yield
    t = seg(OFF_GV, OFF_GG)
    put("gv", t)
    rotary_piece(1, rq, rk, t)
    yield
    g_low = seg(OFF_GLOW, D_IN_PAD).astype(BF16)
    logits = jnp.dot(g_low, w2_ref[...], preferred_element_type=F32) + gb_ref[...]
    log_g = ((jnp.minimum(logits, 0.0) - jnp.log1p(jnp.exp(-jnp.abs(logits))))
             * (LOG2E / GLA_GATE_NORMALIZER))
    rotary_piece(2, rq, rk, logits)
    yield
    t = seg(OFF_RG, OFF_GQ)
    put("rgate", _silu(t))
    rotary_piece(3, rq, rk, t)
    yield
    put("gq", seg(OFF_GQ, OFF_GK) * (GLA_KEY_DIM ** -0.5))
    put("gk", seg(OFF_GK, OFF_GV))
    row = lax.broadcasted_iota(jnp.int32, (rows, GLA_KW), 0) & (GLA_CHUNK - 1)
    c = log_g
    step = 1
    while step < GLA_CHUNK:
        c = c + jnp.where(row >= step, pltpu.roll(c, step, 0), 0.0)
        step *= 2
    put("cum", c)
    yield
    put("ggate", _silu(seg(OFF_GG, OFF_GLOW)))


def _mix_chains(st, rtab_ref, rcdec_ref, rnw_ref, rnb_ref, gnw_ref, out_ref, rstate_ref, gstate_ref):
    c64, sub = GLA_CHUNK, GLA_SUB
    head0 = lax.broadcasted_iota(jnp.int32, (c64, LANES), 1) < GLA_KEY_DIM
    vhead0 = lax.broadcasted_iota(jnp.int32, (c64, 2 * GLA_VALUE_DIM), 1) < GLA_VALUE_DIM
    row_j = lax.broadcasted_iota(jnp.int32, (c64, LANES), 0)
    col_l = lax.broadcasted_iota(jnp.int32, (c64, LANES), 1) & (c64 - 1)
    sub_shift = sub.bit_length() - 1
    dist = (row_j >> sub_shift) - (col_l >> sub_shift)
    band = [dist == d for d in (1, 2, 3)]
    diag = (dist == 0) & (col_l <= row_j)
    zero_row = jnp.zeros((1, LANES), F32)

    def head_stack(t):
        z = jnp.zeros_like(t)
        return jnp.concatenate([jnp.where(head0, t, z), jnp.where(head0, z, t)], axis=0)

    def retention_chain(b, h):
        sl = slice(h * RET_DIM, (h + 1) * RET_DIM)
        q = st["qr"][b, :, sl]
        k = st["kr"][b, :, sl]
        v = st["rv"][b, :, sl]
        state = rstate_ref[b, h]
        both = _dot_nt(q, jnp.concatenate([k, state.astype(BF16)], axis=0))
        upd = _dot_tn(v, st["kd"][b, :, sl])
        yield
        scores = (both[:, 0:RET_CHUNK] * rtab_ref[h, 0]).astype(BF16)
        o = jnp.dot(scores, v, preferred_element_type=F32) + both[:, RET_CHUNK:] * rtab_ref[h, 1]
        rstate_ref[b, h] = state * rcdec_ref[h] + upd
        yield
        mu = jnp.mean(o, axis=-1, keepdims=True)
        yield
        d = o - mu
        var = jnp.mean(d * d, axis=-1, keepdims=True)
        yield
        y = (d * lax.rsqrt(var + EPS)) * rnw_ref[:, sl] + rnb_ref[:, sl]
        out_ref[b, :, sl] = (y * st["rgate"][b, :, sl]).astype(out_ref.dtype)

    def gla_chunk_front(b, p, r0):
        rows = slice(r0, r0 + c64)
        ksl = slice(p * LANES, (p + 1) * LANES)
        vsl = slice(p * 2 * GLA_VALUE_DIM, (p + 1) * 2 * GLA_VALUE_DIM)
        q = st["gq"][b, rows, ksl]
        k = st["gk"][b, rows, ksl]
        c = st["cum"][b, rows, ksl]
        v = st["gv"][b, rows, vsl]
        ends = [c[(m + 1) * sub - 1:(m + 1) * sub, :] for m in range(c64 // sub)]
        last = ends[-1]
        ref_prev1 = _bcast_rows([zero_row, ends[0], ends[1], ends[2]])
        q1 = q * jnp.exp2(c - ref_prev1)
        q2 = q[2 * sub:] * jnp.exp2(c[2 * sub:] - _bcast_rows([ends[0], ends[1]]))
        q3 = q[3 * sub:] * jnp.exp2(c[3 * sub:] - _bcast_rows([ends[0]]))
        q_in = (q * jnp.exp2(c)).astype(BF16)
        k_band = (k * jnp.exp2(_bcast_rows(ends) - c)).astype(BF16)
        k_diag = (k * jnp.exp2(ref_prev1 - c)).astype(BF16)
        k_out = (k * jnp.exp2(last - c)).astype(BF16)
        lhs = jnp.concatenate([q1, q2, q3], axis=0).astype(BF16)
        rhs = jnp.concatenate([head_stack(k_band), head_stack(k_diag)], axis=0)
        prod = _dot_nt(lhs, rhs)
        upd = _dot_tn(jnp.concatenate([v[:, :GLA_VALUE_DIM], v[:, GLA_VALUE_DIM:]], axis=0), head_stack(k_out))
        zv = jnp.zeros_like(v)
        v_bd = jnp.concatenate([jnp.where(vhead0, v, zv), jnp.where(vhead0, zv, v)], axis=0)
        return dict(rows=rows, prod=prod, upd=upd, v_bd=v_bd, q_in=head_stack(q_in), decay=jnp.exp2(last))

    def gla_chain(b, p):
        chunks = [gla_chunk_front(b, p, cc * c64) for cc in range(SEQ_BLOCK // c64)]
        yield
        state = gstate_ref[b, p]
        outs = []
        for ch in chunks:
            prod = ch["prod"]
            near = jnp.where(diag, prod[0:c64, LANES:], jnp.where(band[0], prod[0:c64, 0:LANES], 0.0))
            far2 = jnp.where(band[1][2 * sub:], prod[c64:c64 + 2 * sub, 0:LANES], 0.0)
            far3 = jnp.where(band[2][3 * sub:], prod[c64 + 2 * sub:, 0:LANES], 0.0)
            attn = jnp.concatenate([near[:2 * sub], near[2 * sub:3 * sub] + far2[:sub],
                                    near[3 * sub:] + far2[sub:] + far3], axis=0).astype(BF16)
            inter = _dot_nt(ch["q_in"], state.astype(BF16))
            outs.append(jnp.dot(attn, ch["v_bd"], preferred_element_type=F32)
                        + jnp.concatenate([inter[:c64], inter[c64:]], axis=1))
            state = state * ch["decay"] + ch["upd"]
        gstate_ref[b, p] = state
        yield
        sums = [[jnp.mean(o[:, hh * GLA_VALUE_DIM:(hh + 1) * GLA_VALUE_DIM] ** 2, axis=-1, keepdims=True)
                 for hh in range(2)] for o in outs]
        yield
        for ch, o, ms in zip(chunks, outs, sums):
            for hh in range(2):
                gsl = slice((2 * p + hh) * GLA_VALUE_DIM, (2 * p + hh + 1) * GLA_VALUE_DIM)
                y = o[:, hh * GLA_VALUE_DIM:(hh + 1) * GLA_VALUE_DIM] * lax.rsqrt(ms[hh] + EPS)
                y = y * gnw_ref[:, gsl] * st["ggate"][b, ch["rows"], gsl]
                out_ref[b, ch["rows"], RET_W + gsl.start:RET_W + gsl.stop] = y.astype(out_ref.dtype)

    chains = []
    for b in range(out_ref.shape[0]):
        late = b * MIX_DELAY
        chains += [_delayed(retention_chain(b, h), late) for h in range(RET_HEADS)]
        chains += [_delayed(gla_chain(b, p), late) for p in range(GLA_PAIRS)]
    return chains


def _projmix_kernel(x_ref, pos_ref, n1w_ref, w_ref, w2_ref, gb_ref, freq_ref, kdec_ref,
                    rtab_ref, rcdec_ref, rnw_ref, rnb_ref, gnw_ref,
                    out_ref, wb_ref, rstate_ref, gstate_ref, *stage_refs):
    n = len(STAGE_NAMES)
    stages = [dict(zip(STAGE_NAMES, stage_refs[:n])), dict(zip(STAGE_NAMES, stage_refs[n:]))]
    step = pl.program_id(0)

    @pl.when(step == 0)
    def _():
        for lo in range(0, OFF_GLOW, PROJ_CAST_COLS):
            wb_ref[:, lo:lo + PROJ_CAST_COLS] = w_ref[:, lo:lo + PROJ_CAST_COLS].astype(BF16)
        wb_ref[:, OFF_GLOW:] = jnp.zeros((wb_ref.shape[0], LANES), BF16)
        wb_ref[:, OFF_GLOW:D_IN_PROJ] = w_ref[:, OFF_GLOW:].astype(BF16)
        rstate_ref[...] = jnp.zeros_like(rstate_ref)
        gstate_ref[...] = jnp.zeros_like(gstate_ref)
        for ref in stages[1].values():
            ref[...] = jnp.zeros_like(ref)

    def body(write_stage, read_stage):
        groups = x_ref.shape[0] // PROJ_BATCHES
        chains = [_delayed(_proj_chain(g, x_ref, pos_ref, n1w_ref, wb_ref, w2_ref, gb_ref, freq_ref, kdec_ref,
                                       write_stage), g) for g in range(groups)]
        chains += _mix_chains(read_stage, rtab_ref, rcdec_ref, rnw_ref, rnb_ref, gnw_ref,
                              out_ref, rstate_ref, gstate_ref)
        _run_interleaved(chains)

    @pl.when(step % 2 == 0)
    def _():
        body(stages[0], stages[1])

    @pl.when(step % 2 == 1)
    def _():
        body(stages[1], stages[0])


def _ffn_kernel(x_ref, mixed_ref, wout_ref, n2w_ref, wup_ref, cw_ref, cb_ref, wdown_ref, fw_ref,
                y_ref, carry_ref, act_ref):
    d_ff = wdown_ref.shape[0]
    n_sub = x_ref.shape[0] // FFN_SUB

    @pl.when(pl.program_id(1) == 0)
    def _():
        carry_ref[...] = jnp.zeros_like(carry_ref)

    groups = FFN_SUB // SUBLANES
    sub_row = lax.broadcasted_iota(jnp.int32, (groups, SUBLANES, FFN_COLS), 1)
    tails = [dict() for _ in range(n_sub)]

    def sub_tile(i):
        rows = pl.ds(i * FFN_SUB, FFN_SUB)
        h = x_ref[rows, :] + jnp.dot(mixed_ref[rows, :], wout_ref[...], preferred_element_type=F32)
        ms = jnp.mean(h * h, axis=-1, keepdims=True)
        xn = (h * lax.rsqrt(ms + EPS) * n2w_ref[...]).astype(BF16)
        yield

        def conv_block(lo):
            cols = slice(lo, lo + FFN_COLS)
            up = jnp.dot(xn, wup_ref[:, cols], preferred_element_type=F32)
            prev = carry_ref[:, cols] if i == 0 else tails[i - 1][lo]
            tails[i][lo] = up[FFN_SUB - CARRY_ROWS:, :]
            if i == n_sub - 1:
                carry_ref[:, cols] = tails[i][lo]
            up3 = up.reshape(groups, SUBLANES, FFN_COLS)
            prev3 = prev.reshape(1, SUBLANES, FFN_COLS)
            shifted = []
            for shift in range(1, CONV_WIDTH):
                rot = pltpu.roll(up3, shift, 1)
                above = jnp.concatenate([pltpu.roll(prev3, shift, 1), rot[:-1]], axis=0)
                shifted.append(jnp.where(sub_row < shift, above, rot).reshape(FFN_SUB, FFN_COLS))
            m1, m2 = shifted
            return (cw_ref[0:1, cols] * m2 + cw_ref[1:2, cols] * m1 + cw_ref[2:3, cols] * up) + cb_ref[:, cols]

        for j in range(d_ff // FFN_COLS):
            gate = conv_block(j * FFN_COLS)
            val = conv_block(d_ff + j * FFN_COLS)
            act_ref[rows, j * FFN_COLS:(j + 1) * FFN_COLS] = (_silu(gate) * val).astype(BF16)
            yield

        h2 = h + jnp.dot(act_ref[rows, :], wdown_ref[...], preferred_element_type=F32)
        ms2 = jnp.mean(h2 * h2, axis=-1, keepdims=True)
        y_ref[rows, :] = (h2 * lax.rsqrt(ms2 + EPS) * fw_ref[...]).astype(y_ref.dtype)

    _run_interleaved([_delayed(sub_tile(i), i) for i in range(n_sub)])


def _retention_tables():
    c = RET_CHUNK
    log_gamma = np.log(1.0 - 2.0 ** (-5.0 - np.arange(RET_HEADS, dtype=np.float64)))
    j = np.arange(c, dtype=np.float64)
    diff = j[:, None] - j[None, :]
    intra = np.where(diff >= 0, np.exp(log_gamma[:, None, None] * np.maximum(diff, 0.0)), 0.0)
    q_dec = np.broadcast_to(np.exp(log_gamma[:, None] * (j + 1.0))[:, :, None], (RET_HEADS, c, c))
    tabs = np.stack([intra, q_dec], axis=1).astype(np.float32)
    chunk_dec = np.broadcast_to(np.exp(log_gamma * c)[:, None, None], (RET_HEADS, 1, c)).astype(np.float32)
    k_dec = np.exp(log_gamma[None, :] * (c - 1.0 - j)[:, None])
    k_dec = np.repeat(k_dec, RET_DIM, axis=1).astype(np.float32)
    return tabs, chunk_dec, k_dec


def _rotary_table():
    half = RET_DIM // 2
    inv_freq = ROPE_BASE ** (-jnp.arange(half, dtype=F32) / half)
    phase = jnp.concatenate([jnp.zeros((half,), F32), jnp.full((half,), math.pi / 2, F32)])
    return jnp.stack([jnp.concatenate([inv_freq, inv_freq]), phase])


def _const_spec(shape):
    zeros = (0,) * len(shape)
    return pl.BlockSpec(shape, lambda *_: zeros)


def _layer(layer, h, positions, norm1_w, w_in, ret_norm_w, ret_norm_b, gla_gate_w2, gla_gate_b, gla_norm_w,
           w_out, norm2_w, ffn_w_up, ffn_conv_w, ffn_conv_b, ffn_w_down, final_w):
    b, s, d_model = h.shape
    t = b * s
    d_ff = ffn_w_down.shape[1]
    assert s % SEQ_BLOCK == 0 and s % FFN_ROWS == 0 and b % PROJ_BATCHES == 0
    assert SEQ_BLOCK == RET_CHUNK and d_ff % FFN_COLS == 0 and w_in.shape[1:] == (d_model, D_IN_PROJ)

    w2_p = jnp.pad(gla_gate_w2[layer], ((0, LANES - GLA_RANK), (0, 0))).astype(BF16)
    row = lambda a: a.reshape(1, -1).astype(F32)
    rtab, rcdec, kdec = _retention_tables()
    n_blocks = s // SEQ_BLOCK

    def ahead(width):
        return pl.BlockSpec((b, SEQ_BLOCK, width), lambda i: (0, jnp.minimum(i, n_blocks - 1), 0))

    mixed = pl.pallas_call(
        _projmix_kernel,
        grid=(n_blocks + 1,),
        in_specs=[ahead(d_model), ahead(1), _const_spec((1, d_model)),
                  pl.BlockSpec((None, d_model, D_IN_PROJ), lambda i: (layer, 0, 0), pipeline_mode=pl.Buffered(1)),
                  _const_spec((LANES, GLA_KW)), _const_spec((1, GLA_KW)), _const_spec((2, LANES)),
                  _const_spec(kdec.shape), _const_spec(rtab.shape), _const_spec(rcdec.shape),
                  _const_spec((1, RET_W)), _const_spec((1, RET_W)), _const_spec((1, GLA_VW))],
        out_specs=pl.BlockSpec((b, SEQ_BLOCK, RET_W + GLA_VW), lambda i: (0, jnp.maximum(i - 1, 0), 0)),
        out_shape=jax.ShapeDtypeStruct((b, s, RET_W + GLA_VW), BF16),
        scratch_shapes=[pltpu.VMEM((d_model, D_IN_PAD), BF16),
                        pltpu.VMEM((b, RET_HEADS, RET_DIM, RET_DIM), F32),
                        pltpu.VMEM((b, GLA_PAIRS, GLA_VALUE_DIM, LANES), F32)]
        + 2 * [pltpu.VMEM((b, SEQ_BLOCK, w), dt) for w, dt in zip(STAGE_WIDTHS, STAGE_DTYPES)],
        compiler_params=pltpu.CompilerParams(dimension_semantics=("arbitrary",), vmem_limit_bytes=VMEM_LIMIT),
        name="projmix",
    )(h, positions.reshape(b, s, 1), row(norm1_w[layer]), w_in.astype(F32), w2_p, row(gla_gate_b[layer]),
      _rotary_table(), jnp.asarray(kdec), jnp.asarray(rtab), jnp.asarray(rcdec),
      row(ret_norm_w[layer]), row(ret_norm_b[layer]), row(gla_norm_w[layer]))
    mixed = mixed.reshape(t, RET_W + GLA_VW)

    fsteps = s // FFN_ROWS

    def ftok(width):
        return pl.BlockSpec((FFN_ROWS, width), lambda bi, si: (bi * fsteps + si, 0))

    def weight(shape):
        zeros = (0,) * len(shape)
        return pl.BlockSpec(shape, lambda *_: zeros, pipeline_mode=pl.Buffered(1))

    y = pl.pallas_call(
        _ffn_kernel,
        grid=(b, fsteps),
        in_specs=[ftok(d_model), ftok(d_model), weight((d_model, d_model)), _const_spec((1, d_model)),
                  weight((d_model, 2 * d_ff)), _const_spec((CONV_WIDTH, 2 * d_ff)), _const_spec((1, 2 * d_ff)),
                  weight((d_ff, d_model)), _const_spec((1, d_model))],
        out_specs=ftok(d_model),
        out_shape=jax.ShapeDtypeStruct((t, d_model), h.dtype),
        scratch_shapes=[pltpu.VMEM((CARRY_ROWS, 2 * d_ff), F32), pltpu.VMEM((FFN_ROWS, d_ff), BF16)],
        compiler_params=pltpu.CompilerParams(dimension_semantics=("parallel", "arbitrary"),
                                             vmem_limit_bytes=VMEM_LIMIT),
        name="ffn",
    )(h.reshape(t, d_model), mixed, w_out[layer].astype(BF16), row(norm2_w[layer]), ffn_w_up[layer].astype(BF16),
      ffn_conv_w[layer].astype(F32), row(ffn_conv_b[layer]), ffn_w_down[layer].astype(BF16), row(final_w))
    return y.reshape(b, s, d_model)


def kernel(x, positions, norm1_w, w_in, ret_norm_w, ret_norm_b, gla_gate_w2, gla_gate_b, gla_norm_w, w_out,
           norm2_w, ffn_w_up, ffn_conv_w, ffn_conv_b, ffn_w_down, final_norm_w):
    depth = w_in.shape[0]
    assert depth == 1, "the final RMSNorm is fused into the (single) layer's ffn kernel"
    return _layer(0, x, positions, norm1_w, w_in, ret_norm_w, ret_norm_b, gla_gate_w2, gla_gate_b, gla_norm_w,
                  w_out, norm2_w, ffn_w_up, ffn_conv_w, ffn_conv_b, ffn_w_down, final_norm_w)
```

```python
import functools
import math

import numpy as np
import jax
import jax.numpy as jnp
from jax import lax
from jax.experimental import pallas as pl
from jax.experimental.pallas import tpu as pltpu

F32 = jnp.float32
BF16 = jnp.bfloat16

LANES = 128
SUBLANES = 8
BF16_ROWS = 16
EPS = 1e-6
ROPE_BASE = 10000.0
LOG2E = math.log2(math.e)

RET_HEADS = 4
RET_DIM = 128
RET_W = RET_HEADS * RET_DIM
RET_CHUNK = 128
GLA_HEADS = 4
GLA_KEY_DIM = 64
GLA_VALUE_DIM = 128
GLA_KW = GLA_HEADS * GLA_KEY_DIM
GLA_VW = GLA_HEADS * GLA_VALUE_DIM
GLA_PAIRS = GLA_HEADS // 2
GLA_RANK = 16
GLA_GATE_NORMALIZER = 16.0
GLA_CHUNK = 64
GLA_SUB = 16
CONV_WIDTH = 3

OFF_RQ, OFF_RK, OFF_RV, OFF_RG = 0, 512, 1024, 1536
OFF_GQ, OFF_GK, OFF_GV, OFF_GG, OFF_GLOW = 2048, 2304, 2560, 3072, 3584
D_IN_PROJ = OFF_GLOW + GLA_RANK
D_IN_PAD = OFF_GLOW + LANES

SEQ_BLOCK = 128
PROJ_BATCHES = 2
PROJ_PIECE = 64
PROJ_CAST_COLS = 512
MIX_DELAY = 0
FFN_ROWS = 512
FFN_SUB = 256
FFN_COLS = 256
CARRY_ROWS = SUBLANES

VMEM_LIMIT = 56 * 1024 * 1024

STAGE_NAMES = ("qr", "kr", "kd", "rv", "rgate", "gq", "gk", "cum", "gv", "ggate")
STAGE_WIDTHS = (RET_W, RET_W, RET_W, RET_W, RET_W, GLA_KW, GLA_KW, GLA_KW, GLA_VW, GLA_VW)
STAGE_DTYPES = (BF16, BF16, BF16, BF16, F32, F32, F32, F32, BF16, F32)


def _silu(v):
    return v * (1.0 / (1.0 + jnp.exp(-v)))


def _dot_nt(a, b):
    return lax.dot_general(a, b, (((1,), (1,)), ((), ())), preferred_element_type=F32)


def _dot_tn(a, b):
    return lax.dot_general(a, b, (((0,), (0,)), ((), ())), preferred_element_type=F32)


def _delayed(chain, rounds):
    for _ in range(rounds):
        yield
    yield from chain


def _run_interleaved(chains):
    while chains:
        chains = [c for c in chains if next(c, StopIteration) is not StopIteration]


def _zero_after(value):
    bits = lax.bitcast_convert_type(value[0:SUBLANES, 0:LANES], jnp.uint32)
    half = jnp.uint32(16)
    return lax.shift_right_logical(lax.shift_right_logical(bits, half), half).astype(F32)[0:1, :]


_HALF_PI_PARTS = (1.5703125, 4.837512969970703125e-4, 7.54978995489188216e-8)
_SIN_POLY = (-1.6666654611e-1, 8.3321608736e-3, -1.9515295891e-4)
_COS_POLY = (4.166664568298827e-2, -1.388731625493765e-3, 2.443315711809948e-5)


def _cos_rotary(x):
    quad = jnp.round(x * (2.0 / math.pi))
    r = ((x - quad * _HALF_PI_PARTS[0]) - quad * _HALF_PI_PARTS[1]) - quad * _HALF_PI_PARTS[2]
    r2 = r * r
    sin_r = r + r * r2 * (_SIN_POLY[0] + r2 * (_SIN_POLY[1] + r2 * _SIN_POLY[2]))
    cos_r = 1.0 - 0.5 * r2 + r2 * r2 * (_COS_POLY[0] + r2 * (_COS_POLY[1] + r2 * _COS_POLY[2]))
    q = quad.astype(jnp.int32)
    swap = (q & 1) == 1
    negate = ((q + 1) & 2) == 2
    val = jnp.where(swap, sin_r, cos_r)
    return jnp.where(negate, -val, val)


def _bcast_rows(parts):
    return jnp.concatenate([jnp.broadcast_to(p, (GLA_SUB, LANES)) for p in parts], axis=0)


def _proj_chain(g, x_ref, pos_ref, n1w_ref, wb_ref, w2_ref, gb_ref, freq_ref, kdec_ref, st):
    b0 = g * PROJ_BATCHES
    rows = PROJ_BATCHES * SEQ_BLOCK
    x = jnp.concatenate([x_ref[b0 + bb] for bb in range(PROJ_BATCHES)], axis=0)
    ms = jnp.mean(x * x, axis=-1, keepdims=True)
    xn = (x * lax.rsqrt(ms + EPS) * n1w_ref[...]).astype(BF16)

    def seg(lo, hi):
        return jnp.dot(xn, wb_ref[:, lo:hi], preferred_element_type=F32)

    def put(name, val):
        for bb in range(PROJ_BATCHES):
            st[name][b0 + bb] = val[bb * SEQ_BLOCK:(bb + 1) * SEQ_BLOCK].astype(st[name].dtype)

    low = lax.broadcasted_iota(jnp.int32, (PROJ_PIECE, LANES), 1) < RET_DIM // 2
    per_block = SEQ_BLOCK // PROJ_PIECE

    def rotary_piece(i, rq, rk, after):
        bb, part = divmod(i, per_block)
        prow = slice(part * PROJ_PIECE, (part + 1) * PROJ_PIECE)
        ang = (pos_ref[b0 + bb, prow, :].astype(F32) * freq_ref[0:1, :] - freq_ref[1:2, :]
               + _zero_after(after))
        cs = _cos_rotary(ang)
        sc = pltpu.roll(cs, RET_DIM // 2, 1)
        cos_full = jnp.where(low, cs, sc)
        sin_signed = jnp.where(low, -sc, cs)
        for h in range(RET_HEADS):
            sl = slice(h * RET_DIM, (h + 1) * RET_DIM)
            tq = rq[i * PROJ_PIECE:(i + 1) * PROJ_PIECE, sl]
            tk = rk[i * PROJ_PIECE:(i + 1) * PROJ_PIECE, sl]
            st["qr"][b0 + bb, prow, sl] = (tq * cos_full
                                           + pltpu.roll(tq, RET_DIM // 2, 1) * sin_signed).astype(BF16)
            k_rot = (tk * cos_full + pltpu.roll(tk, RET_DIM // 2, 1) * sin_signed) * (RET_DIM ** -0.5)
            st["kr"][b0 + bb, prow, sl] = k_rot.astype(BF16)
            st["kd"][b0 + bb, prow, sl] = (k_rot * kdec_ref[prow, sl]).astype(BF16)

    yield
    rq = seg(OFF_RQ, OFF_RK)
    yield
    rk = seg(OFF_RK, OFF_RV)
    yield
    t = seg(OFF_RV, OFF_RG)
    put("rv", t)
    rotary_piece(0, rq, rk, t)
    yield
    t = seg(OFF_GV, OFF_GG)
    put("gv", t)
    rotary_piece(1, rq, rk, t)
    yield
    g_low = seg(OFF_GLOW, D_IN_PAD).astype(BF16)
    logits = jnp.dot(g_low, w2_ref[...], preferred_element_type=F32) + gb_ref[...]
    log_g = ((jnp.minimum(logits, 0.0) - jnp.log1p(jnp.exp(-jnp.abs(logits))))
             * (LOG2E / GLA_GATE_NORMALIZER))
    rotary_piece(2, rq, rk, logits)
    yield
    t = seg(OFF_RG, OFF_GQ)
    put("rgate", _silu(t))
    rotary_piece(3, rq, rk, t)
    yield
    put("gq", seg(OFF_GQ, OFF_GK) * (GLA_KEY_DIM ** -0.5))
    put("gk", seg(OFF_GK, OFF_GV))
    row = lax.broadcasted_iota(jnp.int32, (rows, GLA_KW), 0) & (GLA_CHUNK - 1)
    c = log_g
    step = 1
    while step < GLA_CHUNK:
        c = c + jnp.where(row >= step, pltpu.roll(c, step, 0), 0.0)
        step *= 2
    put("cum", c)
    yield
    put("ggate", _silu(seg(OFF_GG, OFF_GLOW)))


def _mix_chains(st, rtab_ref, rcdec_ref, rnw_ref, rnb_ref, gnw_ref, out_ref, rstate_ref, gstate_ref):
    c64, sub = GLA_CHUNK, GLA_SUB
    head0 = lax.broadcasted_iota(jnp.int32, (c64, LANES), 1) < GLA_KEY_DIM
    vhead0 = lax.broadcasted_iota(jnp.int32, (c64, 2 * GLA_VALUE_DIM), 1) < GLA_VALUE_DIM
    row_j = lax.broadcasted_iota(jnp.int32, (c64, LANES), 0)
    col_l = lax.broadcasted_iota(jnp.int32, (c64, LANES), 1) & (c64 - 1)
    sub_shift = sub.bit_length() - 1
    dist = (row_j >> sub_shift) - (col_l >> sub_shift)
    band = [dist == d for d in (1, 2, 3)]
    diag = (dist == 0) & (col_l <= row_j)
    zero_row = jnp.zeros((1, LANES), F32)

    def head_stack(t):
        z = jnp.zeros_like(t)
        return jnp.concatenate([jnp.where(head0, t, z), jnp.where(head0, z, t)], axis=0)

    def retention_chain(b, h):
        sl = slice(h * RET_DIM, (h + 1) * RET_DIM)
        q = st["qr"][b, :, sl]
        k = st["kr"][b, :, sl]
        v = st["rv"][b, :, sl]
        state = rstate_ref[b, h]
        both = _dot_nt(q, jnp.concatenate([k, state.astype(BF16)], axis=0))
        upd = _dot_tn(v, st["kd"][b, :, sl])
        yield
        scores = (both[:, 0:RET_CHUNK] * rtab_ref[h, 0]).astype(BF16)
        o = jnp.dot(scores, v, preferred_element_type=F32) + both[:, RET_CHUNK:] * rtab_ref[h, 1]
        rstate_ref[b, h] = state * rcdec_ref[h] + upd
        yield
        mu = jnp.mean(o, axis=-1, keepdims=True)
        yield
        d = o - mu
        var = jnp.mean(d * d, axis=-1, keepdims=True)
        yield
        y = (d * lax.rsqrt(var + EPS)) * rnw_ref[:, sl] + rnb_ref[:, sl]
        out_ref[b, :, sl] = (y * st["rgate"][b, :, sl]).astype(out_ref.dtype)

    def gla_chunk_front(b, p, r0):
        rows = slice(r0, r0 + c64)
        ksl = slice(p * LANES, (p + 1) * LANES)
        vsl = slice(p * 2 * GLA_VALUE_DIM, (p + 1) * 2 * GLA_VALUE_DIM)
        q = st["gq"][b, rows, ksl]
        k = st["gk"][b, rows, ksl]
        c = st["cum"][b, rows, ksl]
        v = st["gv"][b, rows, vsl]
        ends = [c[(m + 1) * sub - 1:(m + 1) * sub, :] for m in range(c64 // sub)]
        last = ends[-1]
        ref_prev1 = _bcast_rows([zero_row, ends[0], ends[1], ends[2]])
        q1 = q * jnp.exp2(c - ref_prev1)
        q2 = q[2 * sub:] * jnp.exp2(c[2 * sub:] - _bcast_rows([ends[0], ends[1]]))
        q3 = q[3 * sub:] * jnp.exp2(c[3 * sub:] - _bcast_rows([ends[0]]))
        q_in = (q * jnp.exp2(c)).astype(BF16)
        k_band = (k * jnp.exp2(_bcast_rows(ends) - c)).astype(BF16)
        k_diag = (k * jnp.exp2(ref_prev1 - c)).astype(BF16)
        k_out = (k * jnp.exp2(last - c)).astype(BF16)
        lhs = jnp.concatenate([q1, q2, q3], axis=0).astype(BF16)
        rhs = jnp.concatenate([head_stack(k_band), head_stack(k_diag)], axis=0)
        prod = _dot_nt(lhs, rhs)
        upd = _dot_tn(jnp.concatenate([v[:, :GLA_VALUE_DIM], v[:, GLA_VALUE_DIM:]], axis=0), head_stack(k_out))
        zv = jnp.zeros_like(v)
        v_bd = jnp.concatenate([jnp.where(vhead0, v, zv), jnp.where(vhead0, zv, v)], axis=0)
        return dict(rows=rows, prod=prod, upd=upd, v_bd=v_bd, q_in=head_stack(q_in), decay=jnp.exp2(last))

    def gla_chain(b, p):
        chunks = [gla_chunk_front(b, p, cc * c64) for cc in range(SEQ_BLOCK // c64)]
        yield
        state = gstate_ref[b, p]
        outs = []
        for ch in chunks:
            prod = ch["prod"]
            near = jnp.where(diag, prod[0:c64, LANES:], jnp.where(band[0], prod[0:c64, 0:LANES], 0.0))
            far2 = jnp.where(band[1][2 * sub:], prod[c64:c64 + 2 * sub, 0:LANES], 0.0)
            far3 = jnp.where(band[2][3 * sub:], prod[c64 + 2 * sub:, 0:LANES], 0.0)
            attn = jnp.concatenate([near[:2 * sub], near[2 * sub:3 * sub] + far2[:sub],
                                    near[3 * sub:] + far2[sub:] + far3], axis=0).astype(BF16)
            inter = _dot_nt(ch["q_in"], state.astype(BF16))
            outs.append(jnp.dot(attn, ch["v_bd"], preferred_element_type=F32)
                        + jnp.concatenate([inter[:c64], inter[c64:]], axis=1))
            state = state * ch["decay"] + ch["upd"]
        gstate_ref[b, p] = state
        yield
        sums = [[jnp.mean(o[:, hh * GLA_VALUE_DIM:(hh + 1) * GLA_VALUE_DIM] ** 2, axis=-1, keepdims=True)
                 for hh in range(2)] for o in outs]
        yield
        for ch, o, ms in zip(chunks, outs, sums):
            for hh in range(2):
                gsl = slice((2 * p + hh) * GLA_VALUE_DIM, (2 * p + hh + 1) * GLA_VALUE_DIM)
                y = o[:, hh * GLA_VALUE_DIM:(hh + 1) * GLA_VALUE_DIM] * lax.rsqrt(ms[hh] + EPS)
                y = y * gnw_ref[:, gsl] * st["ggate"][b, ch["rows"], gsl]
                out_ref[b, ch["rows"], RET_W + gsl.start:RET_W + gsl.stop] = y.astype(out_ref.dtype)

    chains = []
    for b in range(out_ref.shape[0]):
        late = b * MIX_DELAY
        chains += [_delayed(retention_chain(b, h), late) for h in range(RET_HEADS)]
        chains += [_delayed(gla_chain(b, p), late) for p in range(GLA_PAIRS)]
    return chains


def _projmix_kernel(down_steps, x_ref, pos_ref, n1w_ref, w_ref, w2_ref, gb_ref, freq_ref, kdec_ref,
                    rtab_ref, rcdec_ref, rnw_ref, rnb_ref, gnw_ref, wout_ref, wup_ref, wdown_ref,
                    out_ref, wout_b_ref, wup_b_ref, wdown_b_ref, wb_ref, rstate_ref, gstate_ref, *stage_refs):
    n = len(STAGE_NAMES)
    stages = [dict(zip(STAGE_NAMES, stage_refs[:n])), dict(zip(STAGE_NAMES, stage_refs[n:]))]
    step = pl.program_id(0)

    @pl.when(step == 0)
    def _():
        for lo in range(0, OFF_GLOW, PROJ_CAST_COLS):
            wb_ref[:, lo:lo + PROJ_CAST_COLS] = w_ref[lo:lo + PROJ_CAST_COLS, :].T.astype(BF16)
        tail = jnp.concatenate([w_ref[OFF_GLOW:, :], jnp.zeros((LANES - GLA_RANK, w_ref.shape[1]), F32)], axis=0)
        wb_ref[:, OFF_GLOW:] = tail.T.astype(BF16)
        rstate_ref[...] = jnp.zeros_like(rstate_ref)
        gstate_ref[...] = jnp.zeros_like(gstate_ref)
        for ref in stages[1].values():
            ref[...] = jnp.zeros_like(ref)

    wout_b_ref[...] = wout_ref[...].astype(BF16)
    wup_b_ref[...] = wup_ref[...].astype(BF16)

    @pl.when(step < down_steps)
    def _():
        wdown_b_ref[...] = wdown_ref[...].astype(BF16)

    def body(write_stage, read_stage):
        groups = x_ref.shape[0] // PROJ_BATCHES
        chains = [_delayed(_proj_chain(g, x_ref, pos_ref, n1w_ref, wb_ref, w2_ref, gb_ref, freq_ref, kdec_ref,
                                       write_stage), g) for g in range(groups)]
        chains += _mix_chains(read_stage, rtab_ref, rcdec_ref, rnw_ref, rnb_ref, gnw_ref,
                              out_ref, rstate_ref, gstate_ref)
        _run_interleaved(chains)

    @pl.when(step % 2 == 0)
    def _():
        body(stages[0], stages[1])

    @pl.when(step % 2 == 1)
    def _():
        body(stages[1], stages[0])


def _ffn_kernel(x_ref, mixed_ref, wout_ref, n2w_ref, wup_ref, cw_ref, cb_ref, wdown_ref, fw_ref,
                y_ref, carry_ref, act_ref):
    d_ff = wdown_ref.shape[0]
    n_sub = x_ref.shape[0] // FFN_SUB

    @pl.when(pl.program_id(1) == 0)
    def _():
        carry_ref[...] = jnp.zeros_like(carry_ref)

    groups = FFN_SUB // SUBLANES
    sub_row = lax.broadcasted_iota(jnp.int32, (groups, SUBLANES, FFN_COLS), 1)
    tails = [dict() for _ in range(n_sub)]

    def sub_tile(i):
        rows = pl.ds(i * FFN_SUB, FFN_SUB)
        h = x_ref[rows, :] + jnp.dot(mixed_ref[rows, :], wout_ref[...], preferred_element_type=F32)
        ms = jnp.mean(h * h, axis=-1, keepdims=True)
        xn = (h * lax.rsqrt(ms + EPS) * n2w_ref[...]).astype(BF16)
        yield

        def conv_block(lo):
            cols = slice(lo, lo + FFN_COLS)
            up = jnp.dot(xn, wup_ref[:, cols], preferred_element_type=F32)
            prev = carry_ref[:, cols] if i == 0 else tails[i - 1][lo]
            tails[i][lo] = up[FFN_SUB - CARRY_ROWS:, :]
            if i == n_sub - 1:
                carry_ref[:, cols] = tails[i][lo]
            up3 = up.reshape(groups, SUBLANES, FFN_COLS)
            prev3 = prev.reshape(1, SUBLANES, FFN_COLS)
            shifted = []
            for shift in range(1, CONV_WIDTH):
                rot = pltpu.roll(up3, shift, 1)
                above = jnp.concatenate([pltpu.roll(prev3, shift, 1), rot[:-1]], axis=0)
                shifted.append(jnp.where(sub_row < shift, above, rot).reshape(FFN_SUB, FFN_COLS))
            m1, m2 = shifted
            return (cw_ref[0:1, cols] * m2 + cw_ref[1:2, cols] * m1 + cw_ref[2:3, cols] * up) + cb_ref[:, cols]

        for j in range(d_ff // FFN_COLS):
            gate = conv_block(j * FFN_COLS)
            val = conv_block(d_ff + j * FFN_COLS)
            act_ref[rows, j * FFN_COLS:(j + 1) * FFN_COLS] = (_silu(gate) * val).astype(BF16)
            yield

        h2 = h + jnp.dot(act_ref[rows, :], wdown_ref[...], preferred_element_type=F32)
        ms2 = jnp.mean(h2 * h2, axis=-1, keepdims=True)
        y_ref[rows, :] = (h2 * lax.rsqrt(ms2 + EPS) * fw_ref[...]).astype(y_ref.dtype)

    _run_interleaved([_delayed(sub_tile(i), i) for i in range(n_sub)])


def _retention_tables():
    c = RET_CHUNK
    log_gamma = np.log(1.0 - 2.0 ** (-5.0 - np.arange(RET_HEADS, dtype=np.float64)))
    j = np.arange(c, dtype=np.float64)
    diff = j[:, None] - j[None, :]
    intra = np.where(diff >= 0, np.exp(log_gamma[:, None, None] * np.maximum(diff, 0.0)), 0.0)
    q_dec = np.broadcast_to(np.exp(log_gamma[:, None] * (j + 1.0))[:, :, None], (RET_HEADS, c, c))
    tabs = np.stack([intra, q_dec], axis=1).astype(np.float32)
    chunk_dec = np.broadcast_to(np.exp(log_gamma * c)[:, None, None], (RET_HEADS, 1, c)).astype(np.float32)
    k_dec = np.exp(log_gamma[None, :] * (c - 1.0 - j)[:, None])
    k_dec = np.repeat(k_dec, RET_DIM, axis=1).astype(np.float32)
    return tabs, chunk_dec, k_dec


def _rotary_table():
    half = RET_DIM // 2
    inv_freq = ROPE_BASE ** (-jnp.arange(half, dtype=F32) / half)
    phase = jnp.concatenate([jnp.zeros((half,), F32), jnp.full((half,), math.pi / 2, F32)])
    return jnp.stack([jnp.concatenate([inv_freq, inv_freq]), phase])


def _const_spec(shape):
    zeros = (0,) * len(shape)
    return pl.BlockSpec(shape, lambda *_: zeros)


def _layer(layer, h, positions, norm1_w, w_in, ret_norm_w, ret_norm_b, gla_gate_w2, gla_gate_b, gla_norm_w,
           w_out, norm2_w, ffn_w_up, ffn_conv_w, ffn_conv_b, ffn_w_down, final_w):
    b, s, d_model = h.shape
    t = b * s
    d_ff = ffn_w_down.shape[1]
    assert s % SEQ_BLOCK == 0 and s % FFN_ROWS == 0 and b % PROJ_BATCHES == 0
    assert SEQ_BLOCK == RET_CHUNK and d_ff % FFN_COLS == 0 and w_in.shape[1:] == (d_model, D_IN_PROJ)

    w2_p = jnp.pad(gla_gate_w2[layer], ((0, LANES - GLA_RANK), (0, 0))).astype(BF16)
    row = lambda a: a.reshape(1, -1).astype(F32)
    rtab, rcdec, kdec = _retention_tables()
    n_blocks = s // SEQ_BLOCK

    def ahead(width):
        return pl.BlockSpec((b, SEQ_BLOCK, width), lambda i: (0, jnp.minimum(i, n_blocks - 1), 0))

    cast_rows = d_model // n_blocks
    down_steps = max(k for k in range(1, n_blocks + 1) if d_ff % (k * BF16_ROWS) == 0)
    down_rows = d_ff // down_steps
    assert d_model % (n_blocks * BF16_ROWS) == 0

    def cast_in(rows, width, steps):
        return pl.BlockSpec((None, rows, width), lambda i: (layer, jnp.minimum(i, steps - 1), 0))

    def cast_out(rows, width, steps):
        return pl.BlockSpec((rows, width), lambda i: (jnp.minimum(i, steps - 1), 0))

    mixed, w_out_b, w_up_b, w_down_b = pl.pallas_call(
        functools.partial(_projmix_kernel, down_steps),
        grid=(n_blocks + 1,),
        in_specs=[ahead(d_model), ahead(1), _const_spec((1, d_model)),
                  pl.BlockSpec((None, D_IN_PROJ, d_model), lambda i: (layer, 0, 0), pipeline_mode=pl.Buffered(1)),
                  _const_spec((LANES, GLA_KW)), _const_spec((1, GLA_KW)), _const_spec((2, LANES)),
                  _const_spec(kdec.shape), _const_spec(rtab.shape), _const_spec(rcdec.shape),
                  _const_spec((1, RET_W)), _const_spec((1, RET_W)), _const_spec((1, GLA_VW)),
                  cast_in(cast_rows, d_model, n_blocks), cast_in(cast_rows, 2 * d_ff, n_blocks),
                  cast_in(down_rows, d_model, down_steps)],
        out_specs=[pl.BlockSpec((b, SEQ_BLOCK, RET_W + GLA_VW), lambda i: (0, jnp.maximum(i - 1, 0), 0)),
                   cast_out(cast_rows, d_model, n_blocks), cast_out(cast_rows, 2 * d_ff, n_blocks),
                   cast_out(down_rows, d_model, down_steps)],
        out_shape=[jax.ShapeDtypeStruct((b, s, RET_W + GLA_VW), BF16),
                   jax.ShapeDtypeStruct((d_model, d_model), BF16), jax.ShapeDtypeStruct((d_model, 2 * d_ff), BF16),
                   jax.ShapeDtypeStruct((d_ff, d_model), BF16)],
        scratch_shapes=[pltpu.VMEM((d_model, D_IN_PAD), BF16),
                        pltpu.VMEM((b, RET_HEADS, RET_DIM, RET_DIM), F32),
                        pltpu.VMEM((b, GLA_PAIRS, GLA_VALUE_DIM, LANES), F32)]
        + 2 * [pltpu.VMEM((b, SEQ_BLOCK, w), dt) for w, dt in zip(STAGE_WIDTHS, STAGE_DTYPES)],
        compiler_params=pltpu.CompilerParams(dimension_semantics=("arbitrary",), vmem_limit_bytes=VMEM_LIMIT),
        name="projmix",
    )(h, positions.reshape(b, s, 1), row(norm1_w[layer]), jnp.swapaxes(w_in, 1, 2).astype(F32), w2_p, row(gla_gate_b[layer]),
      _rotary_table(), jnp.asarray(kdec), jnp.asarray(rtab), jnp.asarray(rcdec),
      row(ret_norm_w[layer]), row(ret_norm_b[layer]), row(gla_norm_w[layer]),
      w_out.astype(F32), ffn_w_up.astype(F32), ffn_w_down.astype(F32))
    mixed = mixed.reshape(t, RET_W + GLA_VW)

    fsteps = s // FFN_ROWS

    def ftok(width):
        return pl.BlockSpec((FFN_ROWS, width), lambda bi, si: (bi * fsteps + si, 0))

    def weight(shape):
        zeros = (0,) * len(shape)
        return pl.BlockSpec(shape, lambda *_: zeros, pipeline_mode=pl.Buffered(1))

    y = pl.pallas_call(
        _ffn_kernel,
        grid=(b, fsteps),
        in_specs=[ftok(d_model), ftok(d_model), weight((d_model, d_model)), _const_spec((1, d_model)),
                  weight((d_model, 2 * d_ff)), _const_spec((CONV_WIDTH, 2 * d_ff)), _const_spec((1, 2 * d_ff)),
                  weight((d_ff, d_model)), _const_spec((1, d_model))],
        out_specs=ftok(d_model),
        out_shape=jax.ShapeDtypeStruct((t, d_model), h.dtype),
        scratch_shapes=[pltpu.VMEM((CARRY_ROWS, 2 * d_ff), F32), pltpu.VMEM((FFN_ROWS, d_ff), BF16)],
        compiler_params=pltpu.CompilerParams(dimension_semantics=("parallel", "arbitrary"),
                                             vmem_limit_bytes=VMEM_LIMIT),
        name="ffn",
    )(h.reshape(t, d_model), mixed, w_out_b, row(norm2_w[layer]), w_up_b,
      ffn_conv_w[layer].astype(F32), row(ffn_conv_b[layer]), w_down_b, row(final_w))
    return y.reshape(b, s, d_model)


def kernel(x, positions, norm1_w, w_in, ret_norm_w, ret_norm_b, gla_gate_w2, gla_gate_b, gla_norm_w, w_out,
           norm2_w, ffn_w_up, ffn_conv_w, ffn_conv_b, ffn_w_down, final_norm_w):
    depth = w_in.shape[0]
    assert depth == 1, "the final RMSNorm is fused into the (single) layer's ffn kernel"
    return _layer(0, x, positions, norm1_w, w_in, ret_norm_w, ret_norm_b, gla_gate_w2, gla_gate_b, gla_norm_w,
                  w_out, norm2_w, ffn_w_up, ffn_conv_w, ffn_conv_b, ffn_w_down, final_norm_w)
```

```python
import functools
import math

import numpy as np
import jax
import jax.numpy as jnp
from jax import lax
from jax.experimental import pallas as pl
from jax.experimental.pallas import tpu as pltpu

F32 = jnp.float32
BF16 = jnp.bfloat16

LANES = 128
SUBLANES = 8
BF16_ROWS = 16
EPS = 1e-6
ROPE_BASE = 10000.0
LOG2E = math.log2(math.e)

RET_HEADS = 4
RET_DIM = 128
RET_W = RET_HEADS * RET_DIM
RET_CHUNK = 128
GLA_HEADS = 4
GLA_KEY_DIM = 64
GLA_VALUE_DIM = 128
GLA_KW = GLA_HEADS * GLA_KEY_DIM
GLA_VW = GLA_HEADS * GLA_VALUE_DIM
GLA_PAIRS = GLA_HEADS // 2
GLA_RANK = 16
GLA_GATE_NORMALIZER = 16.0
GLA_CHUNK = 64
GLA_SUB = 16
CONV_WIDTH = 3

OFF_RQ, OFF_RK, OFF_RV, OFF_RG = 0, 512, 1024, 1536
OFF_GQ, OFF_GK, OFF_GV, OFF_GG, OFF_GLOW = 2048, 2304, 2560, 3072, 3584
D_IN_PROJ = OFF_GLOW + GLA_RANK
D_IN_PAD = OFF_GLOW + LANES

SEQ_BLOCK = 128
PROJ_BATCHES = 2
PROJ_PIECE = 64
PROJ_CAST_COLS = 512
MIX_DELAY = 0
FFN_ROWS = 512
FFN_SUB = 256
FFN_COLS = 256
CARRY_ROWS = SUBLANES

VMEM_LIMIT = 56 * 1024 * 1024

STAGE_NAMES = ("qr", "kr", "kd", "rv", "rgate", "gq", "gk", "cum", "gv", "ggate")
STAGE_WIDTHS = (RET_W, RET_W, RET_W, RET_W, RET_W, GLA_KW, GLA_KW, GLA_KW, GLA_VW, GLA_VW)
STAGE_DTYPES = (BF16, BF16, BF16, BF16, F32, F32, F32, F32, BF16, F32)


def _silu(v):
    return v * (1.0 / (1.0 + jnp.exp(-v)))


def _dot_nt(a, b):
    return lax.dot_general(a, b, (((1,), (1,)), ((), ())), preferred_element_type=F32)


def _dot_tn(a, b):
    return lax.dot_general(a, b, (((0,), (0,)), ((), ())), preferred_element_type=F32)


def _delayed(chain, rounds):
    for _ in range(rounds):
        yield
    yield from chain


def _run_interleaved(chains):
    while chains:
        chains = [c for c in chains if next(c, StopIteration) is not StopIteration]


def _zero_after(value):
    bits = lax.bitcast_convert_type(value[0:SUBLANES, 0:LANES], jnp.uint32)
    half = jnp.uint32(16)
    return lax.shift_right_logical(lax.shift_right_logical(bits, half), half).astype(F32)[0:1, :]


_HALF_PI_PARTS = (1.5703125, 4.837512969970703125e-4, 7.54978995489188216e-8)
_SIN_POLY = (-1.6666654611e-1, 8.3321608736e-3, -1.9515295891e-4)
_COS_POLY = (4.166664568298827e-2, -1.388731625493765e-3, 2.443315711809948e-5)


def _cos_rotary(x):
    quad = jnp.round(x * (2.0 / math.pi))
    r = ((x - quad * _HALF_PI_PARTS[0]) - quad * _HALF_PI_PARTS[1]) - quad * _HALF_PI_PARTS[2]
    r2 = r * r
    sin_r = r + r * r2 * (_SIN_POLY[0] + r2 * (_SIN_POLY[1] + r2 * _SIN_POLY[2]))
    cos_r = 1.0 - 0.5 * r2 + r2 * r2 * (_COS_POLY[0] + r2 * (_COS_POLY[1] + r2 * _COS_POLY[2]))
    q = quad.astype(jnp.int32)
    swap = (q & 1) == 1
    negate = ((q + 1) & 2) == 2
    val = jnp.where(swap, sin_r, cos_r)
    return jnp.where(negate, -val, val)


def _bcast_rows(parts):
    return jnp.concatenate([jnp.broadcast_to(p, (GLA_SUB, LANES)) for p in parts], axis=0)


def _proj_chain(g, x_ref, pos_ref, n1w_ref, wb_ref, w2_ref, gb_ref, freq_ref, kdec_ref, st):
    b0 = g * PROJ_BATCHES
    rows = PROJ_BATCHES * SEQ_BLOCK
    x = jnp.concatenate([x_ref[b0 + bb] for bb in range(PROJ_BATCHES)], axis=0)
    ms = jnp.mean(x * x, axis=-1, keepdims=True)
    xn = (x * lax.rsqrt(ms + EPS) * n1w_ref[...]).astype(BF16)

    def seg(lo, hi):
        return jnp.dot(xn, wb_ref[:, lo:hi], preferred_element_type=F32)

    def put(name, val):
        for bb in range(PROJ_BATCHES):
            st[name][b0 + bb] = val[bb * SEQ_BLOCK:(bb + 1) * SEQ_BLOCK].astype(st[name].dtype)

    low = lax.broadcasted_iota(jnp.int32, (PROJ_PIECE, LANES), 1) < RET_DIM // 2
    per_block = SEQ_BLOCK // PROJ_PIECE
    pos_rows = pos_ref[...].astype(F32)
    pad = jnp.zeros((LANES - pos_rows.shape[0], SEQ_BLOCK), F32)
    pos_cols = jnp.concatenate([pos_rows, pad], axis=0).T

    def rotary_piece(i, rq, rk, after):
        bb, part = divmod(i, per_block)
        prow = slice(part * PROJ_PIECE, (part + 1) * PROJ_PIECE)
        ang = (pos_cols[prow, b0 + bb:b0 + bb + 1] * freq_ref[0:1, :] - freq_ref[1:2, :]
               + _zero_after(after))
        cs = _cos_rotary(ang)
        sc = pltpu.roll(cs, RET_DIM // 2, 1)
        cos_full = jnp.where(low, cs, sc)
        sin_signed = jnp.where(low, -sc, cs)
        for h in range(RET_HEADS):
            sl = slice(h * RET_DIM, (h + 1) * RET_DIM)
            tq = rq[i * PROJ_PIECE:(i + 1) * PROJ_PIECE, sl]
            tk = rk[i * PROJ_PIECE:(i + 1) * PROJ_PIECE, sl]
            st["qr"][b0 + bb, prow, sl] = (tq * cos_full
                                           + pltpu.roll(tq, RET_DIM // 2, 1) * sin_signed).astype(BF16)
            k_rot = (tk * cos_full + pltpu.roll(tk, RET_DIM // 2, 1) * sin_signed) * (RET_DIM ** -0.5)
            st["kr"][b0 + bb, prow, sl] = k_rot.astype(BF16)
            st["kd"][b0 + bb, prow, sl] = (k_rot * kdec_ref[prow, sl]).astype(BF16)

    yield
    rq = seg(OFF_RQ, OFF_RK)
    yield
    rk = seg(OFF_RK, OFF_RV)
    yield
    t = seg(OFF_RV, OFF_RG)
    put("rv", t)
    rotary_piece(0, rq, rk, t)
    yield
    t = seg(OFF_GV, OFF_GG)
    put("gv", t)
    rotary_piece(1, rq, rk, t)
    yield
    g_low = seg(OFF_GLOW, D_IN_PAD).astype(BF16)
    logits = jnp.dot(g_low, w2_ref[...], preferred_element_type=F32) + gb_ref[...]
    log_g = ((jnp.minimum(logits, 0.0) - jnp.log1p(jnp.exp(-jnp.abs(logits))))
             * (LOG2E / GLA_GATE_NORMALIZER))
    rotary_piece(2, rq, rk, logits)
    yield
    t = seg(OFF_RG, OFF_GQ)
    put("rgate", _silu(t))
    rotary_piece(3, rq, rk, t)
    yield
    put("gq", seg(OFF_GQ, OFF_GK) * (GLA_KEY_DIM ** -0.5))
    put("gk", seg(OFF_GK, OFF_GV))
    row = lax.broadcasted_iota(jnp.int32, (rows, GLA_KW), 0) & (GLA_CHUNK - 1)
    c = log_g
    step = 1
    while step < GLA_CHUNK:
        c = c + jnp.where(row >= step, pltpu.roll(c, step, 0), 0.0)
        step *= 2
    put("cum", c)
    yield
    put("ggate", _silu(seg(OFF_GG, OFF_GLOW)))


def _mix_chains(st, rtab_ref, rcdec_ref, rnw_ref, rnb_ref, gnw_ref, out_ref, rstate_ref, gstate_ref):
    c64, sub = GLA_CHUNK, GLA_SUB
    head0 = lax.broadcasted_iota(jnp.int32, (c64, LANES), 1) < GLA_KEY_DIM
    vhead0 = lax.broadcasted_iota(jnp.int32, (c64, 2 * GLA_VALUE_DIM), 1) < GLA_VALUE_DIM
    row_j = lax.broadcasted_iota(jnp.int32, (c64, LANES), 0)
    col_l = lax.broadcasted_iota(jnp.int32, (c64, LANES), 1) & (c64 - 1)
    sub_shift = sub.bit_length() - 1
    dist = (row_j >> sub_shift) - (col_l >> sub_shift)
    band = [dist == d for d in (1, 2, 3)]
    diag = (dist == 0) & (col_l <= row_j)
    zero_row = jnp.zeros((1, LANES), F32)

    def head_stack(t):
        z = jnp.zeros_like(t)
        return jnp.concatenate([jnp.where(head0, t, z), jnp.where(head0, z, t)], axis=0)

    def retention_chain(b, h):
        sl = slice(h * RET_DIM, (h + 1) * RET_DIM)
        q = st["qr"][b, :, sl]
        k = st["kr"][b, :, sl]
        v = st["rv"][b, :, sl]
        state = rstate_ref[b, h]
        both = _dot_nt(q, jnp.concatenate([k, state.astype(BF16)], axis=0))
        upd = _dot_tn(v, st["kd"][b, :, sl])
        yield
        scores = (both[:, 0:RET_CHUNK] * rtab_ref[h, 0]).astype(BF16)
        o = jnp.dot(scores, v, preferred_element_type=F32) + both[:, RET_CHUNK:] * rtab_ref[h, 1]
        rstate_ref[b, h] = state * rcdec_ref[h] + upd
        yield
        mu = jnp.mean(o, axis=-1, keepdims=True)
        yield
        d = o - mu
        var = jnp.mean(d * d, axis=-1, keepdims=True)
        yield
        y = (d * lax.rsqrt(var + EPS)) * rnw_ref[:, sl] + rnb_ref[:, sl]
        out_ref[b, :, sl] = (y * st["rgate"][b, :, sl]).astype(out_ref.dtype)

    def gla_chunk_front(b, p, r0):
        rows = slice(r0, r0 + c64)
        ksl = slice(p * LANES, (p + 1) * LANES)
        vsl = slice(p * 2 * GLA_VALUE_DIM, (p + 1) * 2 * GLA_VALUE_DIM)
        q = st["gq"][b, rows, ksl]
        k = st["gk"][b, rows, ksl]
        c = st["cum"][b, rows, ksl]
        v = st["gv"][b, rows, vsl]
        ends = [c[(m + 1) * sub - 1:(m + 1) * sub, :] for m in range(c64 // sub)]
        last = ends[-1]
        ref_prev1 = _bcast_rows([zero_row, ends[0], ends[1], ends[2]])
        q1 = q * jnp.exp2(c - ref_prev1)
        q2 = q[2 * sub:] * jnp.exp2(c[2 * sub:] - _bcast_rows([ends[0], ends[1]]))
        q3 = q[3 * sub:] * jnp.exp2(c[3 * sub:] - _bcast_rows([ends[0]]))
        q_in = (q * jnp.exp2(c)).astype(BF16)
        k_band = (k * jnp.exp2(_bcast_rows(ends) - c)).astype(BF16)
        k_diag = (k * jnp.exp2(ref_prev1 - c)).astype(BF16)
        k_out = (k * jnp.exp2(last - c)).astype(BF16)
        lhs = jnp.concatenate([q1, q2, q3], axis=0).astype(BF16)
        rhs = jnp.concatenate([head_stack(k_band), head_stack(k_diag)], axis=0)
        prod = _dot_nt(lhs, rhs)
        upd = _dot_tn(jnp.concatenate([v[:, :GLA_VALUE_DIM], v[:, GLA_VALUE_DIM:]], axis=0), head_stack(k_out))
        zv = jnp.zeros_like(v)
        v_bd = jnp.concatenate([jnp.where(vhead0, v, zv), jnp.where(vhead0, zv, v)], axis=0)
        return dict(rows=rows, prod=prod, upd=upd, v_bd=v_bd, q_in=head_stack(q_in), decay=jnp.exp2(last))

    def gla_chain(b, p):
        chunks = [gla_chunk_front(b, p, cc * c64) for cc in range(SEQ_BLOCK // c64)]
        yield
        state = gstate_ref[b, p]
        outs = []
        for ch in chunks:
            prod = ch["prod"]
            near = jnp.where(diag, prod[0:c64, LANES:], jnp.where(band[0], prod[0:c64, 0:LANES], 0.0))
            far2 = jnp.where(band[1][2 * sub:], prod[c64:c64 + 2 * sub, 0:LANES], 0.0)
            far3 = jnp.where(band[2][3 * sub:], prod[c64 + 2 * sub:, 0:LANES], 0.0)
            attn = jnp.concatenate([near[:2 * sub], near[2 * sub:3 * sub] + far2[:sub],
                                    near[3 * sub:] + far2[sub:] + far3], axis=0).astype(BF16)
            inter = _dot_nt(ch["q_in"], state.astype(BF16))
            outs.append(jnp.dot(attn, ch["v_bd"], preferred_element_type=F32)
                        + jnp.concatenate([inter[:c64], inter[c64:]], axis=1))
            state = state * ch["decay"] + ch["upd"]
        gstate_ref[b, p] = state
        yield
        sums = [[jnp.mean(o[:, hh * GLA_VALUE_DIM:(hh + 1) * GLA_VALUE_DIM] ** 2, axis=-1, keepdims=True)
                 for hh in range(2)] for o in outs]
        yield
        for ch, o, ms in zip(chunks, outs, sums):
            for hh in range(2):
                gsl = slice((2 * p + hh) * GLA_VALUE_DIM, (2 * p + hh + 1) * GLA_VALUE_DIM)
                y = o[:, hh * GLA_VALUE_DIM:(hh + 1) * GLA_VALUE_DIM] * lax.rsqrt(ms[hh] + EPS)
                y = y * gnw_ref[:, gsl] * st["ggate"][b, ch["rows"], gsl]
                out_ref[b, ch["rows"], RET_W + gsl.start:RET_W + gsl.stop] = y.astype(out_ref.dtype)

    chains = []
    for b in range(out_ref.shape[0]):
        late = b * MIX_DELAY
        chains += [_delayed(retention_chain(b, h), late) for h in range(RET_HEADS)]
        chains += [_delayed(gla_chain(b, p), late) for p in range(GLA_PAIRS)]
    return chains


def _projmix_kernel(down_steps, x_ref, pos_ref, n1w_ref, w_ref, w2_ref, gb_ref, freq_ref, kdec_ref,
                    rtab_ref, rcdec_ref, rnw_ref, rnb_ref, gnw_ref, wout_ref, wup_ref, wdown_ref,
                    out_ref, wout_b_ref, wup_b_ref, wdown_b_ref, wb_ref, rstate_ref, gstate_ref, *stage_refs):
    n = len(STAGE_NAMES)
    stages = [dict(zip(STAGE_NAMES, stage_refs[:n])), dict(zip(STAGE_NAMES, stage_refs[n:]))]
    step = pl.program_id(0)

    @pl.when(step == 0)
    def _():
        for lo in range(0, OFF_GLOW, PROJ_CAST_COLS):
            wb_ref[:, lo:lo + PROJ_CAST_COLS] = w_ref[lo:lo + PROJ_CAST_COLS, :].T.astype(BF16)
        tail = jnp.concatenate([w_ref[OFF_GLOW:, :], jnp.zeros((LANES - GLA_RANK, w_ref.shape[1]), F32)], axis=0)
        wb_ref[:, OFF_GLOW:] = tail.T.astype(BF16)
        rstate_ref[...] = jnp.zeros_like(rstate_ref)
        gstate_ref[...] = jnp.zeros_like(gstate_ref)
        for ref in stages[1].values():
            ref[...] = jnp.zeros_like(ref)

    wout_b_ref[...] = wout_ref[...].astype(BF16)
    wup_b_ref[...] = wup_ref[...].astype(BF16)

    @pl.when(step < down_steps)
    def _():
        wdown_b_ref[...] = wdown_ref[...].astype(BF16)

    def body(write_stage, read_stage):
        groups = x_ref.shape[0] // PROJ_BATCHES
        chains = [_delayed(_proj_chain(g, x_ref, pos_ref, n1w_ref, wb_ref, w2_ref, gb_ref, freq_ref, kdec_ref,
                                       write_stage), g) for g in range(groups)]
        chains += _mix_chains(read_stage, rtab_ref, rcdec_ref, rnw_ref, rnb_ref, gnw_ref,
                              out_ref, rstate_ref, gstate_ref)
        _run_interleaved(chains)

    @pl.when(step % 2 == 0)
    def _():
        body(stages[0], stages[1])

    @pl.when(step % 2 == 1)
    def _():
        body(stages[1], stages[0])


def _ffn_kernel(x_ref, mixed_ref, wout_ref, n2w_ref, wup_ref, cw_ref, cb_ref, wdown_ref, fw_ref,
                y_ref, carry_ref, act_ref):
    d_ff = wdown_ref.shape[0]
    n_sub = x_ref.shape[0] // FFN_SUB

    @pl.when(pl.program_id(1) == 0)
    def _():
        carry_ref[...] = jnp.zeros_like(carry_ref)

    groups = FFN_SUB // SUBLANES
    sub_row = lax.broadcasted_iota(jnp.int32, (groups, SUBLANES, FFN_COLS), 1)
    tails = [dict() for _ in range(n_sub)]

    def sub_tile(i):
        rows = pl.ds(i * FFN_SUB, FFN_SUB)
        h = x_ref[rows, :] + jnp.dot(mixed_ref[rows, :], wout_ref[...], preferred_element_type=F32)
        ms = jnp.mean(h * h, axis=-1, keepdims=True)
        xn = (h * lax.rsqrt(ms + EPS) * n2w_ref[...]).astype(BF16)
        yield

        def up_block(lo):
            return jnp.dot(xn, wup_ref[:, lo:lo + FFN_COLS], preferred_element_type=F32)

        def conv_block(lo, up):
            cols = slice(lo, lo + FFN_COLS)
            prev = carry_ref[:, cols] if i == 0 else tails[i - 1][lo]
            tails[i][lo] = up[FFN_SUB - CARRY_ROWS:, :]
            if i == n_sub - 1:
                carry_ref[:, cols] = tails[i][lo]
            up3 = up.reshape(groups, SUBLANES, FFN_COLS)
            prev3 = prev.reshape(1, SUBLANES, FFN_COLS)
            shifted = []
            for shift in range(1, CONV_WIDTH):
                rot = pltpu.roll(up3, shift, 1)
                above = jnp.concatenate([pltpu.roll(prev3, shift, 1), rot[:-1]], axis=0)
                shifted.append(jnp.where(sub_row < shift, above, rot).reshape(FFN_SUB, FFN_COLS))
            m1, m2 = shifted
            return (cw_ref[0:1, cols] * m2 + cw_ref[1:2, cols] * m1 + cw_ref[2:3, cols] * up) + cb_ref[:, cols]

        for j in range(d_ff // FFN_COLS):
            up_g = up_block(j * FFN_COLS)
            yield
            gate = conv_block(j * FFN_COLS, up_g)
            up_v = up_block(d_ff + j * FFN_COLS)
            yield
            val = conv_block(d_ff + j * FFN_COLS, up_v)
            act_ref[rows, j * FFN_COLS:(j + 1) * FFN_COLS] = (_silu(gate) * val).astype(BF16)

        h2 = h + jnp.dot(act_ref[rows, :], wdown_ref[...], preferred_element_type=F32)
        ms2 = jnp.mean(h2 * h2, axis=-1, keepdims=True)
        y_ref[rows, :] = (h2 * lax.rsqrt(ms2 + EPS) * fw_ref[...]).astype(y_ref.dtype)

    _run_interleaved([_delayed(sub_tile(i), i) for i in range(n_sub)])


def _retention_tables():
    c = RET_CHUNK
    log_gamma = np.log(1.0 - 2.0 ** (-5.0 - np.arange(RET_HEADS, dtype=np.float64)))
    j = np.arange(c, dtype=np.float64)
    diff = j[:, None] - j[None, :]
    intra = np.where(diff >= 0, np.exp(log_gamma[:, None, None] * np.maximum(diff, 0.0)), 0.0)
    q_dec = np.broadcast_to(np.exp(log_gamma[:, None] * (j + 1.0))[:, :, None], (RET_HEADS, c, c))
    tabs = np.stack([intra, q_dec], axis=1).astype(np.float32)
    chunk_dec = np.broadcast_to(np.exp(log_gamma * c)[:, None, None], (RET_HEADS, 1, c)).astype(np.float32)
    k_dec = np.exp(log_gamma[None, :] * (c - 1.0 - j)[:, None])
    k_dec = np.repeat(k_dec, RET_DIM, axis=1).astype(np.float32)
    return tabs, chunk_dec, k_dec


def _rotary_table():
    half = RET_DIM // 2
    inv_freq = ROPE_BASE ** (-jnp.arange(half, dtype=F32) / half)
    phase = jnp.concatenate([jnp.zeros((half,), F32), jnp.full((half,), math.pi / 2, F32)])
    return jnp.stack([jnp.concatenate([inv_freq, inv_freq]), phase])


def _const_spec(shape):
    zeros = (0,) * len(shape)
    return pl.BlockSpec(shape, lambda *_: zeros)


def _layer(layer, h, positions, norm1_w, w_in, ret_norm_w, ret_norm_b, gla_gate_w2, gla_gate_b, gla_norm_w,
           w_out, norm2_w, ffn_w_up, ffn_conv_w, ffn_conv_b, ffn_w_down, final_w):
    b, s, d_model = h.shape
    t = b * s
    d_ff = ffn_w_down.shape[1]
    assert s % SEQ_BLOCK == 0 and s % FFN_ROWS == 0 and b % PROJ_BATCHES == 0
    assert SEQ_BLOCK == RET_CHUNK and d_ff % FFN_COLS == 0 and w_in.shape[1:] == (d_model, D_IN_PROJ)

    w2_p = jnp.pad(gla_gate_w2[layer], ((0, LANES - GLA_RANK), (0, 0))).astype(BF16)
    row = lambda a: a.reshape(1, -1).astype(F32)
    rtab, rcdec, kdec = _retention_tables()
    n_blocks = s // SEQ_BLOCK

    def ahead(width):
        return pl.BlockSpec((b, SEQ_BLOCK, width), lambda i: (0, jnp.minimum(i, n_blocks - 1), 0))

    cast_rows = d_model // n_blocks
    down_steps = max(k for k in range(1, n_blocks + 1) if d_ff % (k * BF16_ROWS) == 0)
    down_rows = d_ff // down_steps
    assert d_model % (n_blocks * BF16_ROWS) == 0

    def cast_in(rows, width, steps):
        return pl.BlockSpec((None, rows, width), lambda i: (layer, jnp.minimum(i, steps - 1), 0))

    def cast_out(rows, width, steps):
        return pl.BlockSpec((rows, width), lambda i: (jnp.minimum(i, steps - 1), 0))

    mixed, w_out_b, w_up_b, w_down_b = pl.pallas_call(
        functools.partial(_projmix_kernel, down_steps),
        grid=(n_blocks + 1,),
        in_specs=[ahead(d_model), pl.BlockSpec((b, SEQ_BLOCK), lambda i: (0, jnp.minimum(i, n_blocks - 1))),
                  _const_spec((1, d_model)),
                  pl.BlockSpec((None, D_IN_PROJ, d_model), lambda i: (layer, 0, 0), pipeline_mode=pl.Buffered(1)),
                  _const_spec((LANES, GLA_KW)), _const_spec((1, GLA_KW)), _const_spec((2, LANES)),
                  _const_spec(kdec.shape), _const_spec(rtab.shape), _const_spec(rcdec.shape),
                  _const_spec((1, RET_W)), _const_spec((1, RET_W)), _const_spec((1, GLA_VW)),
                  cast_in(cast_rows, d_model, n_blocks), cast_in(cast_rows, 2 * d_ff, n_blocks),
                  cast_in(down_rows, d_model, down_steps)],
        out_specs=[pl.BlockSpec((b, SEQ_BLOCK, RET_W + GLA_VW), lambda i: (0, jnp.maximum(i - 1, 0), 0)),
                   cast_out(cast_rows, d_model, n_blocks), cast_out(cast_rows, 2 * d_ff, n_blocks),
                   cast_out(down_rows, d_model, down_steps)],
        out_shape=[jax.ShapeDtypeStruct((b, s, RET_W + GLA_VW), BF16),
                   jax.ShapeDtypeStruct((d_model, d_model), BF16), jax.ShapeDtypeStruct((d_model, 2 * d_ff), BF16),
                   jax.ShapeDtypeStruct((d_ff, d_model), BF16)],
        scratch_shapes=[pltpu.VMEM((d_model, D_IN_PAD), BF16),
                        pltpu.VMEM((b, RET_HEADS, RET_DIM, RET_DIM), F32),
                        pltpu.VMEM((b, GLA_PAIRS, GLA_VALUE_DIM, LANES), F32)]
        + 2 * [pltpu.VMEM((b, SEQ_BLOCK, w), dt) for w, dt in zip(STAGE_WIDTHS, STAGE_DTYPES)],
        compiler_params=pltpu.CompilerParams(dimension_semantics=("arbitrary",), vmem_limit_bytes=VMEM_LIMIT),
        name="projmix",
    )(h, positions, row(norm1_w[layer]), jnp.swapaxes(w_in, 1, 2).astype(F32), w2_p, row(gla_gate_b[layer]),
      _rotary_table(), jnp.asarray(kdec), jnp.asarray(rtab), jnp.asarray(rcdec),
      row(ret_norm_w[layer]), row(ret_norm_b[layer]), row(gla_norm_w[layer]),
      w_out.astype(F32), ffn_w_up.astype(F32), ffn_w_down.astype(F32))
    mixed = mixed.reshape(t, RET_W + GLA_VW)

    fsteps = s // FFN_ROWS

    def ftok(width):
        return pl.BlockSpec((FFN_ROWS, width), lambda bi, si: (bi * fsteps + si, 0))

    def weight(shape):
        zeros = (0,) * len(shape)
        return pl.BlockSpec(shape, lambda *_: zeros, pipeline_mode=pl.Buffered(1))

    y = pl.pallas_call(
        _ffn_kernel,
        grid=(b, fsteps),
        in_specs=[ftok(d_model), ftok(d_model), weight((d_model, d_model)), _const_spec((1, d_model)),
                  weight((d_model, 2 * d_ff)), _const_spec((CONV_WIDTH, 2 * d_ff)), _const_spec((1, 2 * d_ff)),
                  weight((d_ff, d_model)), _const_spec((1, d_model))],
        out_specs=ftok(d_model),
        out_shape=jax.ShapeDtypeStruct((t, d_model), h.dtype),
        scratch_shapes=[pltpu.VMEM((CARRY_ROWS, 2 * d_ff), F32), pltpu.VMEM((FFN_ROWS, d_ff), BF16)],
        compiler_params=pltpu.CompilerParams(dimension_semantics=("parallel", "arbitrary"),
                                             vmem_limit_bytes=VMEM_LIMIT),
        name="ffn",
    )(h.reshape(t, d_model), mixed, w_out_b, row(norm2_w[layer]), w_up_b,
      ffn_conv_w[layer].astype(F32), row(ffn_conv_b[layer]), w_down_b, row(final_w))
    return y.reshape(b, s, d_model)


def kernel(x, positions, norm1_w, w_in, ret_norm_w, ret_norm_b, gla_gate_w2, gla_gate_b, gla_norm_w, w_out,
           norm2_w, ffn_w_up, ffn_conv_w, ffn_conv_b, ffn_w_down, final_norm_w):
    depth = w_in.shape[0]
    assert depth == 1, "the final RMSNorm is fused into the (single) layer's ffn kernel"
    return _layer(0, x, positions, norm1_w, w_in, ret_norm_w, ret_norm_b, gla_gate_w2, gla_gate_b, gla_norm_w,
                  w_out, norm2_w, ffn_w_up, ffn_conv_w, ffn_conv_b, ffn_w_down, final_norm_w)
```

```python
import functools
import math

import numpy as np
import jax
import jax.numpy as jnp
from jax import lax
from jax.experimental import pallas as pl
from jax.experimental.pallas import tpu as pltpu

F32 = jnp.float32
BF16 = jnp.bfloat16

LANES = 128
SUBLANES = 8
BF16_ROWS = 16
EPS = 1e-6
ROPE_BASE = 10000.0
LOG2E = math.log2(math.e)

RET_HEADS = 4
RET_DIM = 128
RET_W = RET_HEADS * RET_DIM
RET_CHUNK = 128
GLA_HEADS = 4
GLA_KEY_DIM = 64
GLA_VALUE_DIM = 128
GLA_KW = GLA_HEADS * GLA_KEY_DIM
GLA_VW = GLA_HEADS * GLA_VALUE_DIM
GLA_PAIRS = GLA_HEADS // 2
GLA_RANK = 16
GLA_GATE_NORMALIZER = 16.0
GLA_CHUNK = 64
GLA_SUB = 16
CONV_WIDTH = 3

OFF_RQ, OFF_RK, OFF_RV, OFF_RG = 0, 512, 1024, 1536
OFF_GQ, OFF_GK, OFF_GV, OFF_GG, OFF_GLOW = 2048, 2304, 2560, 3072, 3584
D_IN_PROJ = OFF_GLOW + GLA_RANK
D_IN_PAD = OFF_GLOW + LANES

SEQ_BLOCK = 128
PROJ_BATCHES = 2
PROJ_PIECE = 64
PROJ_CAST_COLS = 512
MIX_DELAY = 0
FFN_ROWS = 512
FFN_SUB = 256
FFN_COLS = 256
CARRY_ROWS = SUBLANES

VMEM_LIMIT = 56 * 1024 * 1024

STAGE_NAMES = ("qr", "kr", "kd", "rv", "rgate", "gq", "gk", "cum", "gv", "ggate")
STAGE_WIDTHS = (RET_W, RET_W, RET_W, RET_W, RET_W, GLA_KW, GLA_KW, GLA_KW, GLA_VW, GLA_VW)
STAGE_DTYPES = (BF16, BF16, BF16, BF16, F32, F32, F32, F32, BF16, F32)


def _silu(v):
    return v * (1.0 / (1.0 + jnp.exp(-v)))


def _dot_nt(a, b):
    return lax.dot_general(a, b, (((1,), (1,)), ((), ())), preferred_element_type=F32)


def _dot_tn(a, b):
    return lax.dot_general(a, b, (((0,), (0,)), ((), ())), preferred_element_type=F32)


def _delayed(chain, rounds):
    for _ in range(rounds):
        yield
    yield from chain


def _run_interleaved(chains):
    while chains:
        chains = [c for c in chains if next(c, StopIteration) is not StopIteration]


def _zero_after(value):
    bits = lax.bitcast_convert_type(value[0:SUBLANES, 0:LANES], jnp.uint32)
    half = jnp.uint32(16)
    return lax.shift_right_logical(lax.shift_right_logical(bits, half), half).astype(F32)[0:1, :]


_HALF_PI_PARTS = (1.5703125, 4.837512969970703125e-4, 7.54978995489188216e-8)
_SIN_POLY = (-1.6666654611e-1, 8.3321608736e-3, -1.9515295891e-4)
_COS_POLY = (4.166664568298827e-2, -1.388731625493765e-3, 2.443315711809948e-5)


def _cos_rotary(x):
    quad = jnp.round(x * (2.0 / math.pi))
    r = ((x - quad * _HALF_PI_PARTS[0]) - quad * _HALF_PI_PARTS[1]) - quad * _HALF_PI_PARTS[2]
    r2 = r * r
    sin_r = r + r * r2 * (_SIN_POLY[0] + r2 * (_SIN_POLY[1] + r2 * _SIN_POLY[2]))
    cos_r = 1.0 - 0.5 * r2 + r2 * r2 * (_COS_POLY[0] + r2 * (_COS_POLY[1] + r2 * _COS_POLY[2]))
    q = quad.astype(jnp.int32)
    swap = (q & 1) == 1
    negate = ((q + 1) & 2) == 2
    val = jnp.where(swap, sin_r, cos_r)
    return jnp.where(negate, -val, val)


def _bcast_rows(parts):
    return jnp.concatenate([jnp.broadcast_to(p, (GLA_SUB, LANES)) for p in parts], axis=0)


def _proj_chain(g, x_ref, pos_ref, n1w_ref, wb_ref, w2_ref, gb_ref, freq_ref, kdec_ref, st):
    b0 = g * PROJ_BATCHES
    rows = PROJ_BATCHES * SEQ_BLOCK
    x = jnp.concatenate([x_ref[b0 + bb] for bb in range(PROJ_BATCHES)], axis=0)
    ms = jnp.mean(x * x, axis=-1, keepdims=True)
    xn = (x * lax.rsqrt(ms + EPS) * n1w_ref[...]).astype(BF16)

    def seg(lo, hi):
        return jnp.dot(xn, wb_ref[:, lo:hi], preferred_element_type=F32)

    def put(name, val):
        for bb in range(PROJ_BATCHES):
            st[name][b0 + bb] = val[bb * SEQ_BLOCK:(bb + 1) * SEQ_BLOCK].astype(st[name].dtype)

    low = lax.broadcasted_iota(jnp.int32, (PROJ_PIECE, LANES), 1) < RET_DIM // 2
    per_block = SEQ_BLOCK // PROJ_PIECE
    pos_rows = pos_ref[...].astype(F32)
    pad = jnp.zeros((LANES - pos_rows.shape[0], SEQ_BLOCK), F32)
    pos_cols = jnp.concatenate([pos_rows, pad], axis=0).T

    def rotary_piece(i, rq, rk, after):
        bb, part = divmod(i, per_block)
        prow = slice(part * PROJ_PIECE, (part + 1) * PROJ_PIECE)
        ang = (pos_cols[prow, b0 + bb:b0 + bb + 1] * freq_ref[0:1, :] - freq_ref[1:2, :]
               + _zero_after(after))
        cs = _cos_rotary(ang)
        sc = pltpu.roll(cs, RET_DIM // 2, 1)
        cos_full = jnp.where(low, cs, sc)
        sin_signed = jnp.where(low, -sc, cs)
        for h in range(RET_HEADS):
            sl = slice(h * RET_DIM, (h + 1) * RET_DIM)
            tq = rq[i * PROJ_PIECE:(i + 1) * PROJ_PIECE, sl]
            tk = rk[i * PROJ_PIECE:(i + 1) * PROJ_PIECE, sl]
            st["qr"][b0 + bb, prow, sl] = (tq * cos_full
                                           + pltpu.roll(tq, RET_DIM // 2, 1) * sin_signed).astype(BF16)
            k_rot = (tk * cos_full + pltpu.roll(tk, RET_DIM // 2, 1) * sin_signed) * (RET_DIM ** -0.5)
            st["kr"][b0 + bb, prow, sl] = k_rot.astype(BF16)
            st["kd"][b0 + bb, prow, sl] = (k_rot * kdec_ref[prow, sl]).astype(BF16)

    assert rows == 4 * PROJ_PIECE, "the four rotary pieces below cover the sub-tile"
    yield
    rq = seg(OFF_RQ, OFF_RK)
    yield
    rk = seg(OFF_RK, OFF_RV)
    yield
    t_rv = seg(OFF_RV, OFF_RG)
    yield
    put("rv", t_rv)
    rotary_piece(0, rq, rk, t_rv)
    t_gv = seg(OFF_GV, OFF_GG)
    yield
    put("gv", t_gv)
    rotary_piece(1, rq, rk, t_gv)
    g_low = seg(OFF_GLOW, D_IN_PAD).astype(BF16)
    logits = jnp.dot(g_low, w2_ref[...], preferred_element_type=F32) + gb_ref[...]
    yield
    log_g = ((jnp.minimum(logits, 0.0) - jnp.log1p(jnp.exp(-jnp.abs(logits))))
             * (LOG2E / GLA_GATE_NORMALIZER))
    rotary_piece(2, rq, rk, logits)
    t_rg = seg(OFF_RG, OFF_GQ)
    yield
    put("rgate", _silu(t_rg))
    rotary_piece(3, rq, rk, t_rg)
    t_gq = seg(OFF_GQ, OFF_GK)
    t_gk = seg(OFF_GK, OFF_GV)
    yield
    put("gq", t_gq * (GLA_KEY_DIM ** -0.5))
    put("gk", t_gk)
    row = lax.broadcasted_iota(jnp.int32, (rows, GLA_KW), 0) & (GLA_CHUNK - 1)
    c = log_g
    step = 1
    while step < GLA_CHUNK:
        c = c + jnp.where(row >= step, pltpu.roll(c, step, 0), 0.0)
        step *= 2
    put("cum", c)
    t_gg = seg(OFF_GG, OFF_GLOW)
    yield
    put("ggate", _silu(t_gg))


def _mix_chains(st, rtab_ref, rcdec_ref, rnw_ref, rnb_ref, gnw_ref, out_ref, rstate_ref, gstate_ref):
    c64, sub = GLA_CHUNK, GLA_SUB
    head0 = lax.broadcasted_iota(jnp.int32, (c64, LANES), 1) < GLA_KEY_DIM
    vhead0 = lax.broadcasted_iota(jnp.int32, (c64, 2 * GLA_VALUE_DIM), 1) < GLA_VALUE_DIM
    row_j = lax.broadcasted_iota(jnp.int32, (c64, LANES), 0)
    col_l = lax.broadcasted_iota(jnp.int32, (c64, LANES), 1) & (c64 - 1)
    sub_shift = sub.bit_length() - 1
    dist = (row_j >> sub_shift) - (col_l >> sub_shift)
    band = [dist == d for d in (1, 2, 3)]
    diag = (dist == 0) & (col_l <= row_j)
    zero_row = jnp.zeros((1, LANES), F32)

    def head_stack(t):
        z = jnp.zeros_like(t)
        return jnp.concatenate([jnp.where(head0, t, z), jnp.where(head0, z, t)], axis=0)

    def retention_chain(b, h):
        sl = slice(h * RET_DIM, (h + 1) * RET_DIM)
        q = st["qr"][b, :, sl]
        k = st["kr"][b, :, sl]
        v = st["rv"][b, :, sl]
        state = rstate_ref[b, h]
        both = _dot_nt(q, jnp.concatenate([k, state.astype(BF16)], axis=0))
        upd = _dot_tn(v, st["kd"][b, :, sl])
        yield
        scores = (both[:, 0:RET_CHUNK] * rtab_ref[h, 0]).astype(BF16)
        o = jnp.dot(scores, v, preferred_element_type=F32) + both[:, RET_CHUNK:] * rtab_ref[h, 1]
        rstate_ref[b, h] = state * rcdec_ref[h] + upd
        yield
        mu = jnp.mean(o, axis=-1, keepdims=True)
        yield
        d = o - mu
        var = jnp.mean(d * d, axis=-1, keepdims=True)
        yield
        y = (d * lax.rsqrt(var + EPS)) * rnw_ref[:, sl] + rnb_ref[:, sl]
        out_ref[b, :, sl] = (y * st["rgate"][b, :, sl]).astype(out_ref.dtype)

    def gla_chunk_front(b, p, r0):
        rows = slice(r0, r0 + c64)
        ksl = slice(p * LANES, (p + 1) * LANES)
        vsl = slice(p * 2 * GLA_VALUE_DIM, (p + 1) * 2 * GLA_VALUE_DIM)
        q = st["gq"][b, rows, ksl]
        k = st["gk"][b, rows, ksl]
        c = st["cum"][b, rows, ksl]
        v = st["gv"][b, rows, vsl]
        ends = [c[(m + 1) * sub - 1:(m + 1) * sub, :] for m in range(c64 // sub)]
        last = ends[-1]
        ref_prev1 = _bcast_rows([zero_row, ends[0], ends[1], ends[2]])
        q1 = q * jnp.exp2(c - ref_prev1)
        q2 = q[2 * sub:] * jnp.exp2(c[2 * sub:] - _bcast_rows([ends[0], ends[1]]))
        q3 = q[3 * sub:] * jnp.exp2(c[3 * sub:] - _bcast_rows([ends[0]]))
        q_in = (q * jnp.exp2(c)).astype(BF16)
        k_band = (k * jnp.exp2(_bcast_rows(ends) - c)).astype(BF16)
        k_diag = (k * jnp.exp2(ref_prev1 - c)).astype(BF16)
        k_out = (k * jnp.exp2(last - c)).astype(BF16)
        lhs = jnp.concatenate([q1, q2, q3], axis=0).astype(BF16)
        rhs = jnp.concatenate([head_stack(k_band), head_stack(k_diag)], axis=0)
        prod = _dot_nt(lhs, rhs)
        upd = _dot_tn(jnp.concatenate([v[:, :GLA_VALUE_DIM], v[:, GLA_VALUE_DIM:]], axis=0), head_stack(k_out))
        zv = jnp.zeros_like(v)
        v_bd = jnp.concatenate([jnp.where(vhead0, v, zv), jnp.where(vhead0, zv, v)], axis=0)
        return dict(rows=rows, prod=prod, upd=upd, v_bd=v_bd, q_in=head_stack(q_in), decay=jnp.exp2(last))

    def gla_chain(b, p):
        chunks = [gla_chunk_front(b, p, cc * c64) for cc in range(SEQ_BLOCK // c64)]
        yield
        state = gstate_ref[b, p]
        outs = []
        for ch in chunks:
            prod = ch["prod"]
            near = jnp.where(diag, prod[0:c64, LANES:], jnp.where(band[0], prod[0:c64, 0:LANES], 0.0))
            far2 = jnp.where(band[1][2 * sub:], prod[c64:c64 + 2 * sub, 0:LANES], 0.0)
            far3 = jnp.where(band[2][3 * sub:], prod[c64 + 2 * sub:, 0:LANES], 0.0)
            attn = jnp.concatenate([near[:2 * sub], near[2 * sub:3 * sub] + far2[:sub],
                                    near[3 * sub:] + far2[sub:] + far3], axis=0).astype(BF16)
            inter = _dot_nt(ch["q_in"], state.astype(BF16))
            outs.append(jnp.dot(attn, ch["v_bd"], preferred_element_type=F32)
                        + jnp.concatenate([inter[:c64], inter[c64:]], axis=1))
            state = state * ch["decay"] + ch["upd"]
        gstate_ref[b, p] = state
        yield
        sums = [[jnp.mean(o[:, hh * GLA_VALUE_DIM:(hh + 1) * GLA_VALUE_DIM] ** 2, axis=-1, keepdims=True)
                 for hh in range(2)] for o in outs]
        yield
        for ch, o, ms in zip(chunks, outs, sums):
            for hh in range(2):
                gsl = slice((2 * p + hh) * GLA_VALUE_DIM, (2 * p + hh + 1) * GLA_VALUE_DIM)
                y = o[:, hh * GLA_VALUE_DIM:(hh + 1) * GLA_VALUE_DIM] * lax.rsqrt(ms[hh] + EPS)
                y = y * gnw_ref[:, gsl] * st["ggate"][b, ch["rows"], gsl]
                out_ref[b, ch["rows"], RET_W + gsl.start:RET_W + gsl.stop] = y.astype(out_ref.dtype)

    chains = []
    for b in range(out_ref.shape[0]):
        late = b * MIX_DELAY
        chains += [_delayed(retention_chain(b, h), late) for h in range(RET_HEADS)]
        chains += [_delayed(gla_chain(b, p), late) for p in range(GLA_PAIRS)]
    return chains


def _projmix_kernel(down_steps, x_ref, pos_ref, n1w_ref, w_ref, w2_ref, gb_ref, freq_ref, kdec_ref,
                    rtab_ref, rcdec_ref, rnw_ref, rnb_ref, gnw_ref, wout_ref, wup_ref, wdown_ref,
                    out_ref, wout_b_ref, wup_b_ref, wdown_b_ref, wb_ref, rstate_ref, gstate_ref, *stage_refs):
    n = len(STAGE_NAMES)
    stages = [dict(zip(STAGE_NAMES, stage_refs[:n])), dict(zip(STAGE_NAMES, stage_refs[n:]))]
    step = pl.program_id(0)

    @pl.when(step == 0)
    def _():
        for lo in range(0, OFF_GLOW, PROJ_CAST_COLS):
            wb_ref[:, lo:lo + PROJ_CAST_COLS] = w_ref[lo:lo + PROJ_CAST_COLS, :].T.astype(BF16)
        tail = jnp.concatenate([w_ref[OFF_GLOW:, :], jnp.zeros((LANES - GLA_RANK, w_ref.shape[1]), F32)], axis=0)
        wb_ref[:, OFF_GLOW:] = tail.T.astype(BF16)
        rstate_ref[...] = jnp.zeros_like(rstate_ref)
        gstate_ref[...] = jnp.zeros_like(gstate_ref)
        for ref in stages[1].values():
            ref[...] = jnp.zeros_like(ref)

    wout_b_ref[...] = wout_ref[...].astype(BF16)
    wup_b_ref[...] = wup_ref[...].astype(BF16)

    @pl.when(step < down_steps)
    def _():
        wdown_b_ref[...] = wdown_ref[...].astype(BF16)

    def body(write_stage, read_stage):
        groups = x_ref.shape[0] // PROJ_BATCHES
        chains = [_delayed(_proj_chain(g, x_ref, pos_ref, n1w_ref, wb_ref, w2_ref, gb_ref, freq_ref, kdec_ref,
                                       write_stage), g) for g in range(groups)]
        chains += _mix_chains(read_stage, rtab_ref, rcdec_ref, rnw_ref, rnb_ref, gnw_ref,
                              out_ref, rstate_ref, gstate_ref)
        _run_interleaved(chains)

    @pl.when(step % 2 == 0)
    def _():
        body(stages[0], stages[1])

    @pl.when(step % 2 == 1)
    def _():
        body(stages[1], stages[0])


def _ffn_kernel(x_ref, mixed_ref, wout_ref, n2w_ref, wup_ref, cw_ref, cb_ref, wdown_ref, fw_ref,
                y_ref, carry_ref, act_ref):
    d_ff = wdown_ref.shape[0]
    n_sub = x_ref.shape[0] // FFN_SUB

    @pl.when(pl.program_id(1) == 0)
    def _():
        carry_ref[...] = jnp.zeros_like(carry_ref)

    groups = FFN_SUB // SUBLANES
    sub_row = lax.broadcasted_iota(jnp.int32, (groups, SUBLANES, FFN_COLS), 1)
    tails = [dict() for _ in range(n_sub)]

    def sub_tile(i):
        rows = pl.ds(i * FFN_SUB, FFN_SUB)
        h = x_ref[rows, :] + jnp.dot(mixed_ref[rows, :], wout_ref[...], preferred_element_type=F32)
        ms = jnp.mean(h * h, axis=-1, keepdims=True)
        xn = (h * lax.rsqrt(ms + EPS) * n2w_ref[...]).astype(BF16)
        yield

        def up_block(lo):
            return jnp.dot(xn, wup_ref[:, lo:lo + FFN_COLS], preferred_element_type=F32)

        def conv_block(lo, up):
            cols = slice(lo, lo + FFN_COLS)
            prev = carry_ref[:, cols] if i == 0 else tails[i - 1][lo]
            tails[i][lo] = up[FFN_SUB - CARRY_ROWS:, :]
            if i == n_sub - 1:
                carry_ref[:, cols] = tails[i][lo]
            up3 = up.reshape(groups, SUBLANES, FFN_COLS)
            prev3 = prev.reshape(1, SUBLANES, FFN_COLS)
            shifted = []
            for shift in range(1, CONV_WIDTH):
                rot = pltpu.roll(up3, shift, 1)
                above = jnp.concatenate([pltpu.roll(prev3, shift, 1), rot[:-1]], axis=0)
                shifted.append(jnp.where(sub_row < shift, above, rot).reshape(FFN_SUB, FFN_COLS))
            m1, m2 = shifted
            return (cw_ref[0:1, cols] * m2 + cw_ref[1:2, cols] * m1 + cw_ref[2:3, cols] * up) + cb_ref[:, cols]

        for j in range(d_ff // FFN_COLS):
            up_g = up_block(j * FFN_COLS)
            yield
            gate = conv_block(j * FFN_COLS, up_g)
            up_v = up_block(d_ff + j * FFN_COLS)
            yield
            val = conv_block(d_ff + j * FFN_COLS, up_v)
            act_ref[rows, j * FFN_COLS:(j + 1) * FFN_COLS] = (_silu(gate) * val).astype(BF16)

        h2 = h + jnp.dot(act_ref[rows, :], wdown_ref[...], preferred_element_type=F32)
        ms2 = jnp.mean(h2 * h2, axis=-1, keepdims=True)
        y_ref[rows, :] = (h2 * lax.rsqrt(ms2 + EPS) * fw_ref[...]).astype(y_ref.dtype)

    _run_interleaved([_delayed(sub_tile(i), i) for i in range(n_sub)])


def _retention_tables():
    c = RET_CHUNK
    log_gamma = np.log(1.0 - 2.0 ** (-5.0 - np.arange(RET_HEADS, dtype=np.float64)))
    j = np.arange(c, dtype=np.float64)
    diff = j[:, None] - j[None, :]
    intra = np.where(diff >= 0, np.exp(log_gamma[:, None, None] * np.maximum(diff, 0.0)), 0.0)
    q_dec = np.broadcast_to(np.exp(log_gamma[:, None] * (j + 1.0))[:, :, None], (RET_HEADS, c, c))
    tabs = np.stack([intra, q_dec], axis=1).astype(np.float32)
    chunk_dec = np.broadcast_to(np.exp(log_gamma * c)[:, None, None], (RET_HEADS, 1, c)).astype(np.float32)
    k_dec = np.exp(log_gamma[None, :] * (c - 1.0 - j)[:, None])
    k_dec = np.repeat(k_dec, RET_DIM, axis=1).astype(np.float32)
    return tabs, chunk_dec, k_dec


def _rotary_table():
    half = RET_DIM // 2
    inv_freq = ROPE_BASE ** (-jnp.arange(half, dtype=F32) / half)
    phase = jnp.concatenate([jnp.zeros((half,), F32), jnp.full((half,), math.pi / 2, F32)])
    return jnp.stack([jnp.concatenate([inv_freq, inv_freq]), phase])


def _const_spec(shape):
    zeros = (0,) * len(shape)
    return pl.BlockSpec(shape, lambda *_: zeros)


def _layer(layer, h, positions, norm1_w, w_in, ret_norm_w, ret_norm_b, gla_gate_w2, gla_gate_b, gla_norm_w,
           w_out, norm2_w, ffn_w_up, ffn_conv_w, ffn_conv_b, ffn_w_down, final_w):
    b, s, d_model = h.shape
    t = b * s
    d_ff = ffn_w_down.shape[1]
    assert s % SEQ_BLOCK == 0 and s % FFN_ROWS == 0 and b % PROJ_BATCHES == 0
    assert SEQ_BLOCK == RET_CHUNK and d_ff % FFN_COLS == 0 and w_in.shape[1:] == (d_model, D_IN_PROJ)

    w2_p = jnp.pad(gla_gate_w2[layer], ((0, LANES - GLA_RANK), (0, 0))).astype(BF16)
    row = lambda a: a.reshape(1, -1).astype(F32)
    rtab, rcdec, kdec = _retention_tables()
    n_blocks = s // SEQ_BLOCK

    def ahead(width):
        return pl.BlockSpec((b, SEQ_BLOCK, width), lambda i: (0, jnp.minimum(i, n_blocks - 1), 0))

    cast_rows = d_model // n_blocks
    down_steps = max(k for k in range(1, n_blocks + 1) if d_ff % (k * BF16_ROWS) == 0)
    down_rows = d_ff // down_steps
    assert d_model % (n_blocks * BF16_ROWS) == 0

    def cast_in(rows, width, steps):
        return pl.BlockSpec((None, rows, width), lambda i: (layer, jnp.minimum(i, steps - 1), 0))

    def cast_out(rows, width, steps):
        return pl.BlockSpec((rows, width), lambda i: (jnp.minimum(i, steps - 1), 0))

    mixed, w_out_b, w_up_b, w_down_b = pl.pallas_call(
        functools.partial(_projmix_kernel, down_steps),
        grid=(n_blocks + 1,),
        in_specs=[ahead(d_model), pl.BlockSpec((b, SEQ_BLOCK), lambda i: (0, jnp.minimum(i, n_blocks - 1))),
                  _const_spec((1, d_model)),
                  pl.BlockSpec((None, D_IN_PROJ, d_model), lambda i: (layer, 0, 0), pipeline_mode=pl.Buffered(1)),
                  _const_spec((LANES, GLA_KW)), _const_spec((1, GLA_KW)), _const_spec((2, LANES)),
                  _const_spec(kdec.shape), _const_spec(rtab.shape), _const_spec(rcdec.shape),
                  _const_spec((1, RET_W)), _const_spec((1, RET_W)), _const_spec((1, GLA_VW)),
                  cast_in(cast_rows, d_model, n_blocks), cast_in(cast_rows, 2 * d_ff, n_blocks),
                  cast_in(down_rows, d_model, down_steps)],
        out_specs=[pl.BlockSpec((b, SEQ_BLOCK, RET_W + GLA_VW), lambda i: (0, jnp.maximum(i - 1, 0), 0)),
                   cast_out(cast_rows, d_model, n_blocks), cast_out(cast_rows, 2 * d_ff, n_blocks),
                   cast_out(down_rows, d_model, down_steps)],
        out_shape=[jax.ShapeDtypeStruct((b, s, RET_W + GLA_VW), BF16),
                   jax.ShapeDtypeStruct((d_model, d_model), BF16), jax.ShapeDtypeStruct((d_model, 2 * d_ff), BF16),
                   jax.ShapeDtypeStruct((d_ff, d_model), BF16)],
        scratch_shapes=[pltpu.VMEM((d_model, D_IN_PAD), BF16),
                        pltpu.VMEM((b, RET_HEADS, RET_DIM, RET_DIM), F32),
                        pltpu.VMEM((b, GLA_PAIRS, GLA_VALUE_DIM, LANES), F32)]
        + 2 * [pltpu.VMEM((b, SEQ_BLOCK, w), dt) for w, dt in zip(STAGE_WIDTHS, STAGE_DTYPES)],
        compiler_params=pltpu.CompilerParams(dimension_semantics=("arbitrary",), vmem_limit_bytes=VMEM_LIMIT),
        name="projmix",
    )(h, positions, row(norm1_w[layer]), jnp.swapaxes(w_in, 1, 2).astype(F32), w2_p, row(gla_gate_b[layer]),
      _rotary_table(), jnp.asarray(kdec), jnp.asarray(rtab), jnp.asarray(rcdec),
      row(ret_norm_w[layer]), row(ret_norm_b[layer]), row(gla_norm_w[layer]),
      w_out.astype(F32), ffn_w_up.astype(F32), ffn_w_down.astype(F32))
    mixed = mixed.reshape(t, RET_W + GLA_VW)

    fsteps = s // FFN_ROWS

    def ftok(width):
        return pl.BlockSpec((FFN_ROWS, width), lambda bi, si: (bi * fsteps + si, 0))

    def weight(shape):
        zeros = (0,) * len(shape)
        return pl.BlockSpec(shape, lambda *_: zeros, pipeline_mode=pl.Buffered(1))

    y = pl.pallas_call(
        _ffn_kernel,
        grid=(b, fsteps),
        in_specs=[ftok(d_model), ftok(d_model), weight((d_model, d_model)), _const_spec((1, d_model)),
                  weight((d_model, 2 * d_ff)), _const_spec((CONV_WIDTH, 2 * d_ff)), _const_spec((1, 2 * d_ff)),
                  weight((d_ff, d_model)), _const_spec((1, d_model))],
        out_specs=ftok(d_model),
        out_shape=jax.ShapeDtypeStruct((t, d_model), h.dtype),
        scratch_shapes=[pltpu.VMEM((CARRY_ROWS, 2 * d_ff), F32), pltpu.VMEM((FFN_ROWS, d_ff), BF16)],
        compiler_params=pltpu.CompilerParams(dimension_semantics=("parallel", "arbitrary"),
                                             vmem_limit_bytes=VMEM_LIMIT),
        name="ffn",
    )(h.reshape(t, d_model), mixed, w_out_b, row(norm2_w[layer]), w_up_b,
      ffn_conv_w[layer].astype(F32), row(ffn_conv_b[layer]), w_down_b, row(final_w))
    return y.reshape(b, s, d_model)


def kernel(x, positions, norm1_w, w_in, ret_norm_w, ret_norm_b, gla_gate_w2, gla_gate_b, gla_norm_w, w_out,
           norm2_w, ffn_w_up, ffn_conv_w, ffn_conv_b, ffn_w_down, final_norm_w):
    depth = w_in.shape[0]
    assert depth == 1, "the final RMSNorm is fused into the (single) layer's ffn kernel"
    return _layer(0, x, positions, norm1_w, w_in, ret_norm_w, ret_norm_b, gla_gate_w2, gla_gate_b, gla_norm_w,
                  w_out, norm2_w, ffn_w_up, ffn_conv_w, ffn_conv_b, ffn_w_down, final_norm_w)
```

```python
import functools
import math

import numpy as np
import jax
import jax.numpy as jnp
from jax import lax
from jax.experimental import pallas as pl
from jax.experimental.pallas import tpu as pltpu

F32 = jnp.float32
BF16 = jnp.bfloat16

LANES = 128
SUBLANES = 8
BF16_ROWS = 16
EPS = 1e-6
ROPE_BASE = 10000.0
LOG2E = math.log2(math.e)

RET_HEADS = 4
RET_DIM = 128
RET_W = RET_HEADS * RET_DIM
RET_CHUNK = 128
GLA_HEADS = 4
GLA_KEY_DIM = 64
GLA_VALUE_DIM = 128
GLA_KW = GLA_HEADS * GLA_KEY_DIM
GLA_VW = GLA_HEADS * GLA_VALUE_DIM
GLA_PAIRS = GLA_HEADS // 2
GLA_RANK = 16
GLA_GATE_NORMALIZER = 16.0
GLA_CHUNK = 64
GLA_SUB = 16
CONV_WIDTH = 3

OFF_RQ, OFF_RK, OFF_RV, OFF_RG = 0, 512, 1024, 1536
OFF_GQ, OFF_GK, OFF_GV, OFF_GG, OFF_GLOW = 2048, 2304, 2560, 3072, 3584
D_IN_PROJ = OFF_GLOW + GLA_RANK
D_IN_PAD = OFF_GLOW + LANES

SEQ_BLOCK = 128
PROJ_BATCHES = 2
PROJ_PIECE = 64
PROJ_CAST_COLS = 512
MIX_DELAY = 0
FFN_ROWS = 512
FFN_SUB = 256
DOWN_COLS = 256
FFN_COLS = 256
CARRY_ROWS = SUBLANES

VMEM_LIMIT = 56 * 1024 * 1024

STAGE_NAMES = ("qr", "kr", "kd", "rv", "rgate", "gq", "gk", "cum", "gv", "ggate")
STAGE_WIDTHS = (RET_W, RET_W, RET_W, RET_W, RET_W, GLA_KW, GLA_KW, GLA_KW, GLA_VW, GLA_VW)
STAGE_DTYPES = (BF16, BF16, BF16, BF16, F32, F32, F32, F32, BF16, F32)


def _silu(v):
    return v * (1.0 / (1.0 + jnp.exp(-v)))


def _dot_nt(a, b):
    return lax.dot_general(a, b, (((1,), (1,)), ((), ())), preferred_element_type=F32)


def _dot_tn(a, b):
    return lax.dot_general(a, b, (((0,), (0,)), ((), ())), preferred_element_type=F32)


def _delayed(chain, rounds):
    for _ in range(rounds):
        yield
    yield from chain


def _run_interleaved(chains):
    while chains:
        chains = [c for c in chains if next(c, StopIteration) is not StopIteration]


def _zero_after(value):
    bits = lax.bitcast_convert_type(value[0:SUBLANES, 0:LANES], jnp.uint32)
    half = jnp.uint32(16)
    return lax.shift_right_logical(lax.shift_right_logical(bits, half), half).astype(F32)[0:1, :]


_HALF_PI_PARTS = (1.5703125, 4.837512969970703125e-4, 7.54978995489188216e-8)
_SIN_POLY = (-1.6666654611e-1, 8.3321608736e-3, -1.9515295891e-4)
_COS_POLY = (4.166664568298827e-2, -1.388731625493765e-3, 2.443315711809948e-5)


def _cos_rotary(x):
    quad = jnp.round(x * (2.0 / math.pi))
    r = ((x - quad * _HALF_PI_PARTS[0]) - quad * _HALF_PI_PARTS[1]) - quad * _HALF_PI_PARTS[2]
    r2 = r * r
    sin_r = r + r * r2 * (_SIN_POLY[0] + r2 * (_SIN_POLY[1] + r2 * _SIN_POLY[2]))
    cos_r = 1.0 - 0.5 * r2 + r2 * r2 * (_COS_POLY[0] + r2 * (_COS_POLY[1] + r2 * _COS_POLY[2]))
    q = quad.astype(jnp.int32)
    swap = (q & 1) == 1
    negate = ((q + 1) & 2) == 2
    val = jnp.where(swap, sin_r, cos_r)
    return jnp.where(negate, -val, val)


def _bcast_rows(parts):
    return jnp.concatenate([jnp.broadcast_to(p, (GLA_SUB, LANES)) for p in parts], axis=0)


def _proj_chain(g, x_ref, pos_ref, n1w_ref, wb_ref, w2_ref, gb_ref, freq_ref, kdec_ref, st):
    b0 = g * PROJ_BATCHES
    rows = PROJ_BATCHES * SEQ_BLOCK
    x = jnp.concatenate([x_ref[b0 + bb] for bb in range(PROJ_BATCHES)], axis=0)
    ms = jnp.mean(x * x, axis=-1, keepdims=True)
    xn = (x * lax.rsqrt(ms + EPS) * n1w_ref[...]).astype(BF16)

    def seg(lo, hi):
        return jnp.dot(xn, wb_ref[:, lo:hi], preferred_element_type=F32)

    def put(name, val):
        for bb in range(PROJ_BATCHES):
            st[name][b0 + bb] = val[bb * SEQ_BLOCK:(bb + 1) * SEQ_BLOCK].astype(st[name].dtype)

    low = lax.broadcasted_iota(jnp.int32, (PROJ_PIECE, LANES), 1) < RET_DIM // 2
    per_block = SEQ_BLOCK // PROJ_PIECE
    pos_rows = pos_ref[...].astype(F32)
    pad = jnp.zeros((LANES - pos_rows.shape[0], SEQ_BLOCK), F32)
    pos_cols = jnp.concatenate([pos_rows, pad], axis=0).T

    def rotary_piece(i, rq, rk, after):
        bb, part = divmod(i, per_block)
        prow = slice(part * PROJ_PIECE, (part + 1) * PROJ_PIECE)
        ang = (pos_cols[prow, b0 + bb:b0 + bb + 1] * freq_ref[0:1, :] - freq_ref[1:2, :]
               + _zero_after(after))
        cs = _cos_rotary(ang)
        sc = pltpu.roll(cs, RET_DIM // 2, 1)
        cos_full = jnp.where(low, cs, sc)
        sin_signed = jnp.where(low, -sc, cs)
        for h in range(RET_HEADS):
            sl = slice(h * RET_DIM, (h + 1) * RET_DIM)
            tq = rq[i * PROJ_PIECE:(i + 1) * PROJ_PIECE, sl]
            tk = rk[i * PROJ_PIECE:(i + 1) * PROJ_PIECE, sl]
            st["qr"][b0 + bb, prow, sl] = (tq * cos_full
                                           + pltpu.roll(tq, RET_DIM // 2, 1) * sin_signed).astype(BF16)
            k_rot = (tk * cos_full + pltpu.roll(tk, RET_DIM // 2, 1) * sin_signed) * (RET_DIM ** -0.5)
            st["kr"][b0 + bb, prow, sl] = k_rot.astype(BF16)
            st["kd"][b0 + bb, prow, sl] = (k_rot * kdec_ref[prow, sl]).astype(BF16)

    assert rows == 4 * PROJ_PIECE, "the four rotary pieces below cover the sub-tile"
    yield
    rq = seg(OFF_RQ, OFF_RK)
    yield
    rk = seg(OFF_RK, OFF_RV)
    yield
    t_rv = seg(OFF_RV, OFF_RG)
    yield
    put("rv", t_rv)
    rotary_piece(0, rq, rk, t_rv)
    t_gv = seg(OFF_GV, OFF_GG)
    yield
    put("gv", t_gv)
    rotary_piece(1, rq, rk, t_gv)
    g_low = seg(OFF_GLOW, D_IN_PAD).astype(BF16)
    logits = jnp.dot(g_low, w2_ref[...], preferred_element_type=F32) + gb_ref[...]
    yield
    log_g = ((jnp.minimum(logits, 0.0) - jnp.log1p(jnp.exp(-jnp.abs(logits))))
             * (LOG2E / GLA_GATE_NORMALIZER))
    rotary_piece(2, rq, rk, logits)
    t_rg = seg(OFF_RG, OFF_GQ)
    yield
    put("rgate", _silu(t_rg))
    rotary_piece(3, rq, rk, t_rg)
    t_gq = seg(OFF_GQ, OFF_GK)
    t_gk = seg(OFF_GK, OFF_GV)
    yield
    put("gq", t_gq * (GLA_KEY_DIM ** -0.5))
    put("gk", t_gk)
    row = lax.broadcasted_iota(jnp.int32, (rows, GLA_KW), 0) & (GLA_CHUNK - 1)
    c = log_g
    step = 1
    while step < GLA_CHUNK:
        c = c + jnp.where(row >= step, pltpu.roll(c, step, 0), 0.0)
        step *= 2
    put("cum", c)
    t_gg = seg(OFF_GG, OFF_GLOW)
    yield
    put("ggate", _silu(t_gg))


def _mix_chains(st, rtab_ref, rcdec_ref, rnw_ref, rnb_ref, gnw_ref, out_ref, rstate_ref, gstate_ref):
    c64, sub = GLA_CHUNK, GLA_SUB
    head0 = lax.broadcasted_iota(jnp.int32, (c64, LANES), 1) < GLA_KEY_DIM
    vhead0 = lax.broadcasted_iota(jnp.int32, (c64, 2 * GLA_VALUE_DIM), 1) < GLA_VALUE_DIM
    row_j = lax.broadcasted_iota(jnp.int32, (c64, LANES), 0)
    col_l = lax.broadcasted_iota(jnp.int32, (c64, LANES), 1) & (c64 - 1)
    sub_shift = sub.bit_length() - 1
    dist = (row_j >> sub_shift) - (col_l >> sub_shift)
    band = [dist == d for d in (1, 2, 3)]
    diag = (dist == 0) & (col_l <= row_j)
    zero_row = jnp.zeros((1, LANES), F32)

    def head_stack(t):
        z = jnp.zeros_like(t)
        return jnp.concatenate([jnp.where(head0, t, z), jnp.where(head0, z, t)], axis=0)

    def retention_chain(b, h):
        sl = slice(h * RET_DIM, (h + 1) * RET_DIM)
        q = st["qr"][b, :, sl]
        k = st["kr"][b, :, sl]
        v = st["rv"][b, :, sl]
        state = rstate_ref[b, h]
        both = _dot_nt(q, jnp.concatenate([k, state.astype(BF16)], axis=0))
        upd = _dot_tn(v, st["kd"][b, :, sl])
        yield
        scores = (both[:, 0:RET_CHUNK] * rtab_ref[h, 0]).astype(BF16)
        o = jnp.dot(scores, v, preferred_element_type=F32) + both[:, RET_CHUNK:] * rtab_ref[h, 1]
        rstate_ref[b, h] = state * rcdec_ref[h] + upd
        yield
        mu = jnp.mean(o, axis=-1, keepdims=True)
        yield
        d = o - mu
        var = jnp.mean(d * d, axis=-1, keepdims=True)
        yield
        y = (d * lax.rsqrt(var + EPS)) * rnw_ref[:, sl] + rnb_ref[:, sl]
        out_ref[b, :, sl] = (y * st["rgate"][b, :, sl]).astype(out_ref.dtype)

    def gla_chunk_front(b, p, r0):
        rows = slice(r0, r0 + c64)
        ksl = slice(p * LANES, (p + 1) * LANES)
        vsl = slice(p * 2 * GLA_VALUE_DIM, (p + 1) * 2 * GLA_VALUE_DIM)
        q = st["gq"][b, rows, ksl]
        k = st["gk"][b, rows, ksl]
        c = st["cum"][b, rows, ksl]
        v = st["gv"][b, rows, vsl]
        ends = [c[(m + 1) * sub - 1:(m + 1) * sub, :] for m in range(c64 // sub)]
        last = ends[-1]
        ref_prev1 = _bcast_rows([zero_row, ends[0], ends[1], ends[2]])
        q1 = q * jnp.exp2(c - ref_prev1)
        q2 = q[2 * sub:] * jnp.exp2(c[2 * sub:] - _bcast_rows([ends[0], ends[1]]))
        q3 = q[3 * sub:] * jnp.exp2(c[3 * sub:] - _bcast_rows([ends[0]]))
        q_in = (q * jnp.exp2(c)).astype(BF16)
        k_band = (k * jnp.exp2(_bcast_rows(ends) - c)).astype(BF16)
        k_diag = (k * jnp.exp2(ref_prev1 - c)).astype(BF16)
        k_out = (k * jnp.exp2(last - c)).astype(BF16)
        lhs = jnp.concatenate([q1, q2, q3], axis=0).astype(BF16)
        rhs = jnp.concatenate([head_stack(k_band), head_stack(k_diag)], axis=0)
        prod = _dot_nt(lhs, rhs)
        upd = _dot_tn(jnp.concatenate([v[:, :GLA_VALUE_DIM], v[:, GLA_VALUE_DIM:]], axis=0), head_stack(k_out))
        zv = jnp.zeros_like(v)
        v_bd = jnp.concatenate([jnp.where(vhead0, v, zv), jnp.where(vhead0, zv, v)], axis=0)
        return dict(rows=rows, prod=prod, upd=upd, v_bd=v_bd, q_in=head_stack(q_in), decay=jnp.exp2(last))

    def gla_chain(b, p):
        chunks = [gla_chunk_front(b, p, cc * c64) for cc in range(SEQ_BLOCK // c64)]
        yield
        state = gstate_ref[b, p]
        outs = []
        for ch in chunks:
            prod = ch["prod"]
            near = jnp.where(diag, prod[0:c64, LANES:], jnp.where(band[0], prod[0:c64, 0:LANES], 0.0))
            far2 = jnp.where(band[1][2 * sub:], prod[c64:c64 + 2 * sub, 0:LANES], 0.0)
            far3 = jnp.where(band[2][3 * sub:], prod[c64 + 2 * sub:, 0:LANES], 0.0)
            attn = jnp.concatenate([near[:2 * sub], near[2 * sub:3 * sub] + far2[:sub],
                                    near[3 * sub:] + far2[sub:] + far3], axis=0).astype(BF16)
            inter = _dot_nt(ch["q_in"], state.astype(BF16))
            outs.append(jnp.dot(attn, ch["v_bd"], preferred_element_type=F32)
                        + jnp.concatenate([inter[:c64], inter[c64:]], axis=1))
            state = state * ch["decay"] + ch["upd"]
        gstate_ref[b, p] = state
        yield
        sums = [[jnp.mean(o[:, hh * GLA_VALUE_DIM:(hh + 1) * GLA_VALUE_DIM] ** 2, axis=-1, keepdims=True)
                 for hh in range(2)] for o in outs]
        yield
        for ch, o, ms in zip(chunks, outs, sums):
            for hh in range(2):
                gsl = slice((2 * p + hh) * GLA_VALUE_DIM, (2 * p + hh + 1) * GLA_VALUE_DIM)
                y = o[:, hh * GLA_VALUE_DIM:(hh + 1) * GLA_VALUE_DIM] * lax.rsqrt(ms[hh] + EPS)
                y = y * gnw_ref[:, gsl] * st["ggate"][b, ch["rows"], gsl]
                out_ref[b, ch["rows"], RET_W + gsl.start:RET_W + gsl.stop] = y.astype(out_ref.dtype)

    chains = []
    for b in range(out_ref.shape[0]):
        late = b * MIX_DELAY
        chains += [_delayed(retention_chain(b, h), late) for h in range(RET_HEADS)]
        chains += [_delayed(gla_chain(b, p), late) for p in range(GLA_PAIRS)]
    return chains


def _projmix_kernel(down_steps, x_ref, pos_ref, n1w_ref, w_ref, w2_ref, gb_ref, freq_ref, kdec_ref,
                    rtab_ref, rcdec_ref, rnw_ref, rnb_ref, gnw_ref, wout_ref, wup_ref, wdown_ref,
                    out_ref, wout_b_ref, wup_b_ref, wdown_b_ref, wb_ref, rstate_ref, gstate_ref, *stage_refs):
    slot = pl.program_id(0) % 2
    write_stage = dict(zip(STAGE_NAMES, [r.at[slot] for r in stage_refs]))
    read_stage = dict(zip(STAGE_NAMES, [r.at[1 - slot] for r in stage_refs]))
    step = pl.program_id(0)

    @pl.when(step == 0)
    def _():
        for lo in range(0, OFF_GLOW, PROJ_CAST_COLS):
            wb_ref[:, lo:lo + PROJ_CAST_COLS] = w_ref[lo:lo + PROJ_CAST_COLS, :].T.astype(BF16)
        tail = jnp.concatenate([w_ref[OFF_GLOW:, :], jnp.zeros((LANES - GLA_RANK, w_ref.shape[1]), F32)], axis=0)
        wb_ref[:, OFF_GLOW:] = tail.T.astype(BF16)
        rstate_ref[...] = jnp.zeros_like(rstate_ref)
        gstate_ref[...] = jnp.zeros_like(gstate_ref)
        for ref in stage_refs:
            ref[...] = jnp.zeros_like(ref)

    wout_b_ref[...] = wout_ref[...].astype(BF16)
    wup_b_ref[...] = wup_ref[...].astype(BF16)

    @pl.when(step < down_steps)
    def _():
        wdown_b_ref[...] = wdown_ref[...].astype(BF16)

    groups = x_ref.shape[0] // PROJ_BATCHES
    chains = [_delayed(_proj_chain(g, x_ref, pos_ref, n1w_ref, wb_ref, w2_ref, gb_ref, freq_ref, kdec_ref,
                                   write_stage), g) for g in range(groups)]
    chains += _mix_chains(read_stage, rtab_ref, rcdec_ref, rnw_ref, rnb_ref, gnw_ref,
                          out_ref, rstate_ref, gstate_ref)
    _run_interleaved(chains)


def _ffn_kernel(x_ref, mixed_ref, wout_ref, n2w_ref, wup_ref, cw_ref, cb_ref, wdown_ref, fw_ref,
                y_ref, carry_ref, act_ref):
    d_ff = wdown_ref.shape[0]
    n_sub = x_ref.shape[0] // FFN_SUB

    @pl.when(pl.program_id(1) == 0)
    def _():
        carry_ref[...] = jnp.zeros_like(carry_ref)

    groups = FFN_SUB // SUBLANES
    sub_row = lax.broadcasted_iota(jnp.int32, (groups, SUBLANES, FFN_COLS), 1)
    tails = [dict() for _ in range(n_sub)]

    def sub_tile(i):
        rows = pl.ds(i * FFN_SUB, FFN_SUB)
        h = x_ref[rows, :] + jnp.dot(mixed_ref[rows, :], wout_ref[...], preferred_element_type=F32)
        ms = jnp.mean(h * h, axis=-1, keepdims=True)
        xn = (h * lax.rsqrt(ms + EPS) * n2w_ref[...]).astype(BF16)
        yield

        def up_block(lo):
            return jnp.dot(xn, wup_ref[:, lo:lo + FFN_COLS], preferred_element_type=F32)

        def conv_block(lo, up):
            cols = slice(lo, lo + FFN_COLS)
            prev = carry_ref[:, cols] if i == 0 else tails[i - 1][lo]
            tails[i][lo] = up[FFN_SUB - CARRY_ROWS:, :]
            if i == n_sub - 1:
                carry_ref[:, cols] = tails[i][lo]
            up3 = up.reshape(groups, SUBLANES, FFN_COLS)
            prev3 = prev.reshape(1, SUBLANES, FFN_COLS)
            shifted = []
            for shift in range(1, CONV_WIDTH):
                rot = pltpu.roll(up3, shift, 1)
                above = jnp.concatenate([pltpu.roll(prev3, shift, 1), rot[:-1]], axis=0)
                shifted.append(jnp.where(sub_row < shift, above, rot).reshape(FFN_SUB, FFN_COLS))
            m1, m2 = shifted
            return (cw_ref[0:1, cols] * m2 + cw_ref[1:2, cols] * m1 + cw_ref[2:3, cols] * up) + cb_ref[:, cols]

        for j in range(d_ff // FFN_COLS):
            up_g = up_block(j * FFN_COLS)
            yield
            gate = conv_block(j * FFN_COLS, up_g)
            up_v = up_block(d_ff + j * FFN_COLS)
            yield
            val = conv_block(d_ff + j * FFN_COLS, up_v)
            act_ref[rows, j * FFN_COLS:(j + 1) * FFN_COLS] = (_silu(gate) * val).astype(BF16)

        parts = []
        for lo in range(0, wdown_ref.shape[1], DOWN_COLS):
            yield
            parts.append(jnp.dot(act_ref[rows, :], wdown_ref[:, lo:lo + DOWN_COLS], preferred_element_type=F32))
        h2 = h + jnp.concatenate(parts, axis=1)
        ms2 = jnp.mean(h2 * h2, axis=-1, keepdims=True)
        y_ref[rows, :] = (h2 * lax.rsqrt(ms2 + EPS) * fw_ref[...]).astype(y_ref.dtype)

    _run_interleaved([_delayed(sub_tile(i), i) for i in range(n_sub)])


def _retention_tables():
    c = RET_CHUNK
    log_gamma = np.log(1.0 - 2.0 ** (-5.0 - np.arange(RET_HEADS, dtype=np.float64)))
    j = np.arange(c, dtype=np.float64)
    diff = j[:, None] - j[None, :]
    intra = np.where(diff >= 0, np.exp(log_gamma[:, None, None] * np.maximum(diff, 0.0)), 0.0)
    q_dec = np.broadcast_to(np.exp(log_gamma[:, None] * (j + 1.0))[:, :, None], (RET_HEADS, c, c))
    tabs = np.stack([intra, q_dec], axis=1).astype(np.float32)
    chunk_dec = np.broadcast_to(np.exp(log_gamma * c)[:, None, None], (RET_HEADS, 1, c)).astype(np.float32)
    k_dec = np.exp(log_gamma[None, :] * (c - 1.0 - j)[:, None])
    k_dec = np.repeat(k_dec, RET_DIM, axis=1).astype(np.float32)
    return tabs, chunk_dec, k_dec


def _rotary_table():
    half = RET_DIM // 2
    inv_freq = ROPE_BASE ** (-jnp.arange(half, dtype=F32) / half)
    phase = jnp.concatenate([jnp.zeros((half,), F32), jnp.full((half,), math.pi / 2, F32)])
    return jnp.stack([jnp.concatenate([inv_freq, inv_freq]), phase])


def _const_spec(shape):
    zeros = (0,) * len(shape)
    return pl.BlockSpec(shape, lambda *_: zeros)


def _layer(layer, h, positions, norm1_w, w_in, ret_norm_w, ret_norm_b, gla_gate_w2, gla_gate_b, gla_norm_w,
           w_out, norm2_w, ffn_w_up, ffn_conv_w, ffn_conv_b, ffn_w_down, final_w):
    b, s, d_model = h.shape
    t = b * s
    d_ff = ffn_w_down.shape[1]
    assert s % SEQ_BLOCK == 0 and s % FFN_ROWS == 0 and b % PROJ_BATCHES == 0
    assert SEQ_BLOCK == RET_CHUNK and d_ff % FFN_COLS == 0 and w_in.shape[1:] == (d_model, D_IN_PROJ)

    w2_p = jnp.pad(gla_gate_w2[layer], ((0, LANES - GLA_RANK), (0, 0))).astype(BF16)
    row = lambda a: a.reshape(1, -1).astype(F32)
    rtab, rcdec, kdec = _retention_tables()
    n_blocks = s // SEQ_BLOCK

    def ahead(width):
        return pl.BlockSpec((b, SEQ_BLOCK, width), lambda i: (0, jnp.minimum(i, n_blocks - 1), 0))

    cast_rows = d_model // n_blocks
    down_steps = max(k for k in range(1, n_blocks + 1) if d_ff % (k * BF16_ROWS) == 0)
    down_rows = d_ff // down_steps
    assert d_model % (n_blocks * BF16_ROWS) == 0

    def cast_in(rows, width, steps):
        return pl.BlockSpec((None, rows, width), lambda i: (layer, jnp.minimum(i, steps - 1), 0))

    def cast_out(rows, width, steps):
        return pl.BlockSpec((rows, width), lambda i: (jnp.minimum(i, steps - 1), 0))

    mixed, w_out_b, w_up_b, w_down_b = pl.pallas_call(
        functools.partial(_projmix_kernel, down_steps),
        grid=(n_blocks + 1,),
        in_specs=[ahead(d_model), pl.BlockSpec((b, SEQ_BLOCK), lambda i: (0, jnp.minimum(i, n_blocks - 1))),
                  _const_spec((1, d_model)),
                  pl.BlockSpec((None, D_IN_PROJ, d_model), lambda i: (layer, 0, 0), pipeline_mode=pl.Buffered(1)),
                  _const_spec((LANES, GLA_KW)), _const_spec((1, GLA_KW)), _const_spec((2, LANES)),
                  _const_spec(kdec.shape), _const_spec(rtab.shape), _const_spec(rcdec.shape),
                  _const_spec((1, RET_W)), _const_spec((1, RET_W)), _const_spec((1, GLA_VW)),
                  cast_in(cast_rows, d_model, n_blocks), cast_in(cast_rows, 2 * d_ff, n_blocks),
                  cast_in(down_rows, d_model, down_steps)],
        out_specs=[pl.BlockSpec((b, SEQ_BLOCK, RET_W + GLA_VW), lambda i: (0, jnp.maximum(i - 1, 0), 0)),
                   cast_out(cast_rows, d_model, n_blocks), cast_out(cast_rows, 2 * d_ff, n_blocks),
                   cast_out(down_rows, d_model, down_steps)],
        out_shape=[jax.ShapeDtypeStruct((b, s, RET_W + GLA_VW), BF16),
                   jax.ShapeDtypeStruct((d_model, d_model), BF16), jax.ShapeDtypeStruct((d_model, 2 * d_ff), BF16),
                   jax.ShapeDtypeStruct((d_ff, d_model), BF16)],
        scratch_shapes=[pltpu.VMEM((d_model, D_IN_PAD), BF16),
                        pltpu.VMEM((b, RET_HEADS, RET_DIM, RET_DIM), F32),
                        pltpu.VMEM((b, GLA_PAIRS, GLA_VALUE_DIM, LANES), F32)]
        + [pltpu.VMEM((2, b, SEQ_BLOCK, w), dt) for w, dt in zip(STAGE_WIDTHS, STAGE_DTYPES)],
        compiler_params=pltpu.CompilerParams(dimension_semantics=("arbitrary",), vmem_limit_bytes=VMEM_LIMIT),
        name="projmix",
    )(h, positions, row(norm1_w[layer]), jnp.swapaxes(w_in, 1, 2).astype(F32), w2_p, row(gla_gate_b[layer]),
      _rotary_table(), jnp.asarray(kdec), jnp.asarray(rtab), jnp.asarray(rcdec),
      row(ret_norm_w[layer]), row(ret_norm_b[layer]), row(gla_norm_w[layer]),
      w_out.astype(F32), ffn_w_up.astype(F32), ffn_w_down.astype(F32))
    mixed = mixed.reshape(t, RET_W + GLA_VW)

    fsteps = s // FFN_ROWS

    def ftok(width):
        return pl.BlockSpec((FFN_ROWS, width), lambda bi, si: (bi * fsteps + si, 0))

    def weight(shape):
        zeros = (0,) * len(shape)
        return pl.BlockSpec(shape, lambda *_: zeros, pipeline_mode=pl.Buffered(1))

    y = pl.pallas_call(
        _ffn_kernel,
        grid=(b, fsteps),
        in_specs=[ftok(d_model), ftok(d_model), weight((d_model, d_model)), _const_spec((1, d_model)),
                  weight((d_model, 2 * d_ff)), _const_spec((CONV_WIDTH, 2 * d_ff)), _const_spec((1, 2 * d_ff)),
                  weight((d_ff, d_model)), _const_spec((1, d_model))],
        out_specs=ftok(d_model),
        out_shape=jax.ShapeDtypeStruct((t, d_model), h.dtype),
        scratch_shapes=[pltpu.VMEM((CARRY_ROWS, 2 * d_ff), F32), pltpu.VMEM((FFN_ROWS, d_ff), BF16)],
        compiler_params=pltpu.CompilerParams(dimension_semantics=("parallel", "arbitrary"),
                                             vmem_limit_bytes=VMEM_LIMIT),
        name="ffn",
    )(h.reshape(t, d_model), mixed, w_out_b, row(norm2_w[layer]), w_up_b,
      ffn_conv_w[layer].astype(F32), row(ffn_conv_b[layer]), w_down_b, row(final_w))
    return y.reshape(b, s, d_model)


def kernel(x, positions, norm1_w, w_in, ret_norm_w, ret_norm_b, gla_gate_w2, gla_gate_b, gla_norm_w, w_out,
           norm2_w, ffn_w_up, ffn_conv_w, ffn_conv_b, ffn_w_down, final_norm_w):
    depth = w_in.shape[0]
    assert depth == 1, "the final RMSNorm is fused into the (single) layer's ffn kernel"
    return _layer(0, x, positions, norm1_w, w_in, ret_norm_w, ret_norm_b, gla_gate_w2, gla_gate_b, gla_norm_w,
                  w_out, norm2_w, ffn_w_up, ffn_conv_w, ffn_conv_b, ffn_w_down, final_norm_w)
```
